```python
import jax, jax.numpy as jnp
from jax import lax
import numpy as np

D_MODEL = 2048
BATCH = 4
SEQ = 2048
DEPTH = 4
DEC_BATCH = 128
DEC_SEQ = 8
PAST_LEN = 16384
PAGE_SIZE = 128

N_MIX = (DEPTH + 1) // 2
N_POOL = DEPTH // 2
D_A = D_MODEL // 2
DK_A = 128
H_A = D_A // DK_A
DV_A = D_A // H_A
CHUNK_A = 16
D_B = D_MODEL - D_A
N_B = 64
H_B = D_B // N_B
W_LORA = max(32, int(round(1.8 * D_B ** 0.5 / 32)) * 32)
A_LORA = max(32, int(round(1.8 * D_B ** 0.5 / 32)) * 32)
G_LORA = max(32, int(round(0.6 * D_B ** 0.8 / 32)) * 32)
PB = 3 * D_B + W_LORA + A_LORA + G_LORA
RWKV_GN_EPS = 64e-5
P_IN = 4 * D_A + PB
POOL_WINDOWS = (2, 4, 8, 16)
POOL_GROUPS = len(POOL_WINDOWS)
POOL_C = D_MODEL // POOL_GROUPS
POOL_HIST = max(POOL_WINDOWS) - 1
N_EXPERTS = 32
TOP_K = 4
D_FF = D_MODEL
SWIGLU_LIMIT = 7.0
SWIGLU_ALPHA = 1.702
MOE_BLOCK = 128
DN_ALPHA = (2 * DEPTH) ** 0.25
DN_BETA = (8 * DEPTH) ** -0.25
LN_EPS = 1e-5

kernel_name = "hgrn2_rwkv7_pool_moe_deepnorm_step"

F32 = jnp.float32


def layer_norm(x, g, b):
    xf = x.astype(F32)
    mu = jnp.mean(xf, -1, keepdims=True)
    var = jnp.mean(jnp.square(xf - mu), -1, keepdims=True)
    return ((xf - mu) * lax.rsqrt(var + LN_EPS) * g + b).astype(x.dtype)


def hgrn2_mix(q, f, i, gate, s0, lb, norm_g):
    B, T, _ = q.shape
    q = jax.nn.silu(q.astype(F32))
    z = f.astype(F32)
    lb = lb.astype(F32)
    logf = jnp.log(lb + (1.0 - lb) * jax.nn.sigmoid(z))
    k = (1.0 - lb) * jax.nn.sigmoid(-z)
    v = i.astype(F32)
    tp = -(-T // CHUNK_A) * CHUNK_A
    pad = [(0, 0), (0, tp - T), (0, 0)]
    q, k, v, logf = [jnp.pad(a, pad) for a in (q, k, v, logf)]
    nc = tp // CHUNK_A

    def blocks(a, d):
        return a.reshape(B, nc, CHUNK_A, -1, d).transpose(1, 0, 3, 2, 4)

    mask = jnp.tril(jnp.ones((CHUNK_A, CHUNK_A), bool))

    def step(S, inp):
        qc, kc, vc, gc = inp
        b = jnp.cumsum(gc, axis=2)
        diff = b[:, :, :, None, :] - b[:, :, None, :, :]
        dec = jnp.exp(jnp.where(mask[:, :, None], diff, -jnp.inf))
        att = jnp.einsum('bhtd,bhsd,bhtsd->bhts', qc, kc, dec)
        o = (jnp.einsum('bhts,bhsv->bhtv', att, vc)
             + jnp.einsum('bhtd,bhdv->bhtv', qc * jnp.exp(b), S))
        bl = b[:, :, -1:, :]
        S = (jnp.exp(bl[:, :, 0, :])[..., None] * S
             + jnp.einsum('bhsd,bhsv->bhdv', kc * jnp.exp(bl - b), vc))
        return S, o

    s_t, o = lax.scan(step, s0.astype(F32),
                      (blocks(q, DK_A), blocks(k, DK_A), blocks(v, DV_A), blocks(logf, DK_A)))
    o = o.transpose(1, 0, 3, 2, 4).reshape(B, tp, H_A, DV_A)[:, :T]
    o = o * lax.rsqrt(jnp.mean(o * o, -1, keepdims=True) + LN_EPS) * norm_g.astype(F32)
    o = o.reshape(B, T, D_A) * jax.nn.silu(gate.astype(F32))
    return o, s_t


def rwkv7_mix(p, shift_prev, s0, mu, w0, w2, a0, a2, g2, k_k, k_a, r_k, gn_g, gn_b):
    B, T, _ = p.shape
    p = p.astype(F32)
    prev = jnp.concatenate([shift_prev[:, None, :].astype(F32), p[:, :-1]], axis=1)
    xm = p + (prev - p) * mu.astype(F32)
    r, k, v, wd, ad, gd = jnp.split(
        xm, [D_B, 2 * D_B, 3 * D_B, 3 * D_B + W_LORA, 3 * D_B + W_LORA + A_LORA], axis=-1)
    wlog = -jax.nn.softplus(-(w0 + jnp.tanh(wd) @ w2)) - 0.5
    decay = jnp.exp(-jnp.exp(wlog))
    a = jax.nn.sigmoid(a0 + ad @ a2)
    g = jax.nn.sigmoid(gd) @ g2
    hs = lambda t: t.reshape(B, T, H_B, N_B)
    r, k, v, decay, a = map(hs, (r, k, v, decay, a))
    kk = k * k_k.astype(F32).reshape(H_B, N_B)
    kk = kk / jnp.maximum(jnp.sqrt(jnp.sum(kk * kk, -1, keepdims=True)), 1e-12)
    k = k * (1.0 + (a - 1.0) * k_a.astype(F32).reshape(H_B, N_B))

    def step(S, inp):
        r_t, w_t, k_t, v_t, kk_t, a_t = inp
        sa = jnp.einsum('bhvk,bhk->bhv', S, -kk_t)
        S = (S * w_t[:, :, None, :] + sa[..., None] * (kk_t * a_t)[:, :, None, :]
             + v_t[..., None] * k_t[:, :, None, :])
        return S, jnp.einsum('bhvk,bhk->bhv', S, r_t)

    tm = lambda t: jnp.moveaxis(t, 1, 0)
    s_t, y = lax.scan(step, s0.astype(F32), tuple(map(tm, (r, decay, k, v, kk, a))))
    y = jnp.moveaxis(y, 0, 1)
    mean = jnp.mean(y, -1, keepdims=True)
    var = jnp.mean(jnp.square(y - mean), -1, keepdims=True)
    y = ((y - mean) * lax.rsqrt(var + RWKV_GN_EPS) * gn_g.astype(F32).reshape(H_B, N_B)
         + gn_b.astype(F32).reshape(H_B, N_B))
    y = y + jnp.sum(r * k * r_k.astype(F32), -1, keepdims=True) * v
    return y.reshape(B, T, D_B) * g, s_t, p[:, -1]


def mix_even(x, s_hgrn, s_rwkv, s_shift, w_in, lb, hgrn_g, mu, w0, w2, a0, a2, g2,
             k_k, k_a, r_k, gn_g, gn_b, w_out):
    p = x @ w_in
    qa, fa, ia, ga = [p[..., n * D_A:(n + 1) * D_A] for n in range(4)]
    o_a, s_a = hgrn2_mix(qa, fa, ia, ga, s_hgrn, lb, hgrn_g)
    o_b, s_b, sh = rwkv7_mix(p[..., 4 * D_A:], s_shift, s_rwkv, mu, w0, w2, a0, a2, g2,
                             k_k, k_a, r_k, gn_g, gn_b)
    y = jnp.concatenate([o_a, o_b], axis=-1).astype(x.dtype) @ w_out
    return y, s_a.astype(x.dtype), s_b.astype(x.dtype), sh.astype(x.dtype)


def pool_mix(x, hist, start_pos, w_pool, scale):
    B, T, D = x.shape
    xf = x.astype(F32)
    xx = jnp.concatenate([hist.astype(F32), xf], axis=1)
    cs = jnp.concatenate([jnp.zeros((B, 1, D), F32), jnp.cumsum(xx, axis=1)], axis=1)
    pos = jnp.arange(T) + start_pos
    hi = cs[:, POOL_HIST + 1:]
    outs = []
    for gi, w in enumerate(POOL_WINDOWS):
        sl = slice(gi * POOL_C, (gi + 1) * POOL_C)
        lo = cs[:, POOL_HIST + 1 - w:POOL_HIST + 1 - w + T, sl]
        cnt = jnp.minimum(pos + 1, w).astype(F32)[None, :, None]
        outs.append((hi[..., sl] - lo) / cnt - xf[..., sl])
    u = jnp.stack(outs, axis=2)
    y = jnp.einsum('btgc,gcd->btgd', u, w_pool.astype(F32)).reshape(B, T, D) * scale.astype(F32)
    return y.astype(x.dtype), xx[:, -POOL_HIST:].astype(x.dtype)


def moe_ffn(x, w_router, b_router, w_gu, b_gu, w_dn, b_dn):
    B, T, D = x.shape
    n = B * T
    xt = x.reshape(n, D)
    logits = xt.astype(F32) @ w_router.astype(F32) + b_router.astype(F32)
    top_v, top_i = lax.top_k(logits, TOP_K)
    gates = jax.nn.softmax(top_v, axis=-1)
    m = n * TOP_K
    e_flat = top_i.reshape(m)
    tok_flat = jnp.repeat(jnp.arange(n, dtype=jnp.int32), TOP_K)
    order = jnp.argsort(e_flat)
    e_sorted = e_flat[order]
    counts = jnp.bincount(e_flat, length=N_EXPERTS)
    padded = (counts + MOE_BLOCK - 1) // MOE_BLOCK * MOE_BLOCK
    start = jnp.cumsum(counts) - counts
    pend = jnp.cumsum(padded)
    pstart = pend - padded
    dest = pstart[e_sorted] + jnp.arange(m, dtype=jnp.int32) - start[e_sorted]
    n_blocks = -(-m // MOE_BLOCK) + N_EXPERTS
    rows = n_blocks * MOE_BLOCK
    row_tok = jnp.full((rows,), n, jnp.int32).at[dest].set(tok_flat[order])
    row_gate = jnp.zeros((rows,), F32).at[dest].set(gates.reshape(m)[order])
    block_e = jnp.minimum(
        jnp.searchsorted(pend, jnp.arange(n_blocks, dtype=jnp.int32) * MOE_BLOCK, side='right'),
        N_EXPERTS - 1)
    xpad = jnp.concatenate([xt, jnp.zeros((1, D), xt.dtype)], axis=0)
    xb = xpad[row_tok].reshape(n_blocks, MOE_BLOCK, D)

    def expert_block(args):
        xblk, e = args
        h = xblk @ w_gu[e] + b_gu[e]
        gl = jnp.minimum(h[:, :D_FF], SWIGLU_LIMIT)
        up = jnp.clip(h[:, D_FF:], -SWIGLU_LIMIT, SWIGLU_LIMIT)
        act = gl * jax.nn.sigmoid(SWIGLU_ALPHA * gl) * (up + 1.0)
        return act @ w_dn[e] + b_dn[e]

    yb = lax.map(expert_block, (xb, block_e)).reshape(rows, D)
    y = jax.ops.segment_sum(yb.astype(F32) * row_gate[:, None], row_tok, num_segments=n + 1)[:n]
    return y.astype(x.dtype).reshape(B, T, D)


def setup_inputs(seed: int = 0) -> dict:
    key = jax.random.key(seed)
    ks = iter(jax.random.split(key, 40))
    nrm = lambda shape, s: jax.random.normal(next(ks), shape, F32) * s
    uni = lambda shape, lo, hi: jax.random.uniform(next(ks), shape, F32, lo, hi)
    return {
        "x_prompt": nrm((BATCH, SEQ, D_MODEL), 1.0),
        "x_sample": nrm((DEC_BATCH, DEC_SEQ, D_MODEL), 1.0),
        "state_hgrn": nrm((N_MIX, DEC_BATCH, H_A, DK_A, DV_A), 0.5),
        "state_rwkv": nrm((N_MIX, DEC_BATCH, H_B, N_B, N_B), 0.3),
        "state_rwkv_shift": nrm((N_MIX, DEC_BATCH, PB), 1.0),
        "state_pool": nrm((N_POOL, DEC_BATCH, POOL_HIST, D_MODEL), 1.0),
        "mix_w_in": nrm((N_MIX, D_MODEL, P_IN), D_MODEL ** -0.5),
        "hgrn_lb": nrm((N_MIX, D_A), 1.0),
        "hgrn_norm_g": 1.0 + nrm((N_MIX, DV_A), 0.1),
        "rwkv_mu": uni((N_MIX, PB), 0.0, 1.0),
        "rwkv_w0": uni((N_MIX, D_B), -5.0, 0.5),
        "rwkv_w2": nrm((N_MIX, W_LORA, D_B), 0.5 * W_LORA ** -0.5),
        "rwkv_a0": nrm((N_MIX, D_B), 0.5),
        "rwkv_a2": nrm((N_MIX, A_LORA, D_B), A_LORA ** -0.5),
        "rwkv_g2": nrm((N_MIX, G_LORA, D_B), G_LORA ** -0.5),
        "rwkv_k_k": 0.85 + nrm((N_MIX, D_B), 0.1),
        "rwkv_k_a": 1.0 + nrm((N_MIX, D_B), 0.1),
        "rwkv_r_k": nrm((N_MIX, H_B, N_B), 0.1),
        "rwkv_gn_g": 1.0 + nrm((N_MIX, D_B), 0.1),
        "rwkv_gn_b": nrm((N_MIX, D_B), 0.02),
        "mix_w_out": nrm((N_MIX, D_MODEL, D_MODEL), DN_BETA * D_MODEL ** -0.5),
        "pool_w": nrm((N_POOL, POOL_GROUPS, POOL_C, POOL_C), DN_BETA * POOL_C ** -0.5),
        "pool_scale": 1.0 + nrm((N_POOL, D_MODEL), 0.1),
        "ln1_g": 1.0 + nrm((DEPTH, D_MODEL), 0.05),
        "ln1_b": nrm((DEPTH, D_MODEL), 0.02),
        "ln2_g": 1.0 + nrm((DEPTH, D_MODEL), 0.05),
        "ln2_b": nrm((DEPTH, D_MODEL), 0.02),
        "moe_w_router": nrm((DEPTH, D_MODEL, N_EXPERTS), D_MODEL ** -0.5),
        "moe_b_router": nrm((DEPTH, N_EXPERTS), 0.01),
        "moe_w_gu": nrm((DEPTH, N_EXPERTS, D_MODEL, 2 * D_FF), D_MODEL ** -0.5),
        "moe_b_gu": nrm((DEPTH, N_EXPERTS, 2 * D_FF), 0.02),
        "moe_w_dn": nrm((DEPTH, N_EXPERTS, D_FF, D_MODEL), DN_BETA * D_FF ** -0.5),
        "moe_b_dn": nrm((DEPTH, N_EXPERTS, D_MODEL), 0.02),
    }


def reference(x_prompt, x_sample, state_hgrn, state_rwkv, state_rwkv_shift, state_pool,
              mix_w_in, hgrn_lb, hgrn_norm_g, rwkv_mu, rwkv_w0, rwkv_w2, rwkv_a0, rwkv_a2,
              rwkv_g2, rwkv_k_k, rwkv_k_a, rwkv_r_k, rwkv_gn_g, rwkv_gn_b, mix_w_out,
              pool_w, pool_scale, ln1_g, ln1_b, ln2_g, ln2_b, moe_w_router, moe_b_router,
              moe_w_gu, moe_b_gu, moe_w_dn, moe_b_dn):
    lb_p = jax.nn.softmax(hgrn_lb.astype(F32), axis=0)
    lb_all = jnp.cumsum(lb_p, axis=0) - lb_p[0]
    xp, xs = x_prompt, x_sample
    bp = xp.shape[0]
    hg_p, hg_s, rw_p, rw_s, sh_p, sh_s, pl_p, pl_s = [], [], [], [], [], [], [], []
    for l in range(DEPTH):
        if l % 2 == 0:
            e = l // 2
            w = (mix_w_in[e], lb_all[e], hgrn_norm_g[e], rwkv_mu[e], rwkv_w0[e], rwkv_w2[e],
                 rwkv_a0[e], rwkv_a2[e], rwkv_g2[e], rwkv_k_k[e], rwkv_k_a[e], rwkv_r_k[e],
                 rwkv_gn_g[e], rwkv_gn_b[e], mix_w_out[e])
            hp, a1, a2, a3 = mix_even(xp, jnp.zeros((bp, H_A, DK_A, DV_A), xp.dtype),
                                      jnp.zeros((bp, H_B, N_B, N_B), xp.dtype),
                                      jnp.zeros((bp, PB), xp.dtype), *w)
            hs, b1, b2, b3 = mix_even(xs, state_hgrn[e], state_rwkv[e], state_rwkv_shift[e], *w)
            hg_p.append(a1); rw_p.append(a2); sh_p.append(a3)
            hg_s.append(b1); rw_s.append(b2); sh_s.append(b3)
        else:
            j = l // 2
            hp, c1 = pool_mix(xp, jnp.zeros((bp, POOL_HIST, D_MODEL), xp.dtype), 0,
                              pool_w[j], pool_scale[j])
            hs, c2 = pool_mix(xs, state_pool[j], PAST_LEN, pool_w[j], pool_scale[j])
            pl_p.append(c1); pl_s.append(c2)
        xp = layer_norm(DN_ALPHA * xp + hp, ln1_g[l], ln1_b[l])
        xs = layer_norm(DN_ALPHA * xs + hs, ln1_g[l], ln1_b[l])
        moe_w = (moe_w_router[l], moe_b_router[l], moe_w_gu[l], moe_b_gu[l], moe_w_dn[l], moe_b_dn[l])
        xp = layer_norm(DN_ALPHA * xp + moe_ffn(xp, *moe_w), ln2_g[l], ln2_b[l])
        xs = layer_norm(DN_ALPHA * xs + moe_ffn(xs, *moe_w), ln2_g[l], ln2_b[l])
    return (xp, xs, jnp.stack(hg_p), jnp.stack(hg_s), jnp.stack(rw_p), jnp.stack(rw_s),
            jnp.stack(sh_p), jnp.stack(sh_s), jnp.stack(pl_p), jnp.stack(pl_s))
```

```python
import functools

import jax
import jax.numpy as jnp
from jax import lax
from jax.experimental import pallas as pl
from jax.experimental.pallas import tpu as pltpu

F32 = jnp.float32
BF16 = jnp.bfloat16
HIGHEST = lax.Precision.HIGHEST

CHUNK_A = 16
POOL_WINDOWS = (2, 4, 8, 16)
TOP_K = 4
SWIGLU_LIMIT = 7.0
SWIGLU_ALPHA = 1.702
LN_EPS = 1e-5
RWKV_GN_EPS = 64e-5
PAST_LEN = 16384

VMEM_LIMIT_BYTES = 56 * 1024 * 1024
LANES = 128

MOE_SUPER = 1024
MOE_SUB = 256
MOE_TF = 256
ROUTER_TM = 512
COMBINE_TB = 128
DMA_CHUNK = 256


def _cparams(sem):
    return pltpu.CompilerParams(dimension_semantics=sem, vmem_limit_bytes=VMEM_LIMIT_BYTES)


def _mm_kernel(x_ref, w_ref, o_ref):
    o_ref[...] = jnp.dot(x_ref[...].astype(BF16), w_ref[...].astype(BF16),
                         preferred_element_type=F32)


def matmul(x, w, tm=512, tn=1024):
    m, k = x.shape
    n = w.shape[1]
    tm = min(tm, m)
    tn = min(tn, n)
    return pl.pallas_call(
        _mm_kernel,
        grid=(pl.cdiv(n, tn), pl.cdiv(m, tm)),
        in_specs=[pl.BlockSpec((tm, k), lambda j, i: (i, 0)),
                  pl.BlockSpec((k, tn), lambda j, i: (0, j))],
        out_specs=pl.BlockSpec((tm, tn), lambda j, i: (i, j)),
        out_shape=jax.ShapeDtypeStruct((m, n), F32),
        compiler_params=_cparams(("arbitrary", "arbitrary")),
        name="matmul",
    )(x, w)


def _ln_rows(z, g, b):
    mu = jnp.mean(z, axis=-1, keepdims=True)
    zc = z - mu
    var = jnp.mean(zc * zc, axis=-1, keepdims=True)
    return zc * lax.rsqrt(var + LN_EPS) * g + b


def _proj_ln_kernel(*refs, n_in, alpha):
    x_ref = refs[0]
    a_refs = refs[1:1 + n_in]
    w_refs = refs[1 + n_in:1 + 2 * n_in]
    g_ref, b_ref, o_ref = refs[1 + 2 * n_in:]
    acc = alpha * x_ref[...]
    for a_ref, w_ref in zip(a_refs, w_refs):
        acc = acc + jnp.dot(a_ref[...].astype(BF16), w_ref[...], preferred_element_type=F32)
    o_ref[...] = _ln_rows(acc, g_ref[...], b_ref[...])


def proj_residual_ln(x, acts, weights, g, b, alpha, tm=256):
    m, d = x.shape
    n_in = len(acts)
    in_specs = [pl.BlockSpec((tm, d), lambda i: (i, 0))]
    in_specs += [pl.BlockSpec((tm, a.shape[1]), lambda i: (i, 0)) for a in acts]
    in_specs += [pl.BlockSpec(w.shape, lambda i: (0, 0)) for w in weights]
    in_specs += [pl.BlockSpec((1, d), lambda i: (0, 0))] * 2
    return pl.pallas_call(
        functools.partial(_proj_ln_kernel, n_in=n_in, alpha=alpha),
        grid=(m // tm,),
        in_specs=in_specs,
        out_specs=pl.BlockSpec((tm, d), lambda i: (i, 0)),
        out_shape=jax.ShapeDtypeStruct((m, d), F32),
        compiler_params=_cparams(("arbitrary",)),
        name="proj_residual_ln",
    )(x, *acts, *weights, g.reshape(1, d), b.reshape(1, d))


def _router_kernel(x_ref, w_ref, b_ref, ti_ref, tg_ref, cnt_ref, *, n_exp, top_k):
    i = pl.program_id(0)

    @pl.when(i == 0)
    def _():
        cnt_ref[...] = jnp.zeros_like(cnt_ref)

    logits = jnp.dot(x_ref[...].astype(BF16), w_ref[...].astype(BF16),
                     preferred_element_type=F32) + b_ref[...]
    tm = logits.shape[0]
    lane = lax.broadcasted_iota(jnp.int32, logits.shape, 1)
    work = logits
    firsts, vals, sel = [], [], jnp.zeros(logits.shape, F32)
    for _ in range(top_k):
        mx = jnp.max(work, axis=-1, keepdims=True)
        first = jnp.min(jnp.where(work == mx, lane, n_exp), axis=-1, keepdims=True)
        pick = lane == first
        firsts.append(first)
        vals.append(mx)
        sel = sel + pick.astype(F32)
        work = jnp.where(pick, -jnp.inf, work)
    exps = [jnp.exp(v - vals[0]) for v in vals]
    den = exps[0]
    for e in exps[1:]:
        den = den + e
    row = lax.broadcasted_iota(jnp.int32, (tm, tm), 0)
    col = lax.broadcasted_iota(jnp.int32, (tm, tm), 1)
    tril = jnp.where(row > col, 1.0, 0.0).astype(BF16)
    before = jnp.dot(tril, sel.astype(BF16), preferred_element_type=F32) + cnt_ref[...]
    out_lane = lax.broadcasted_iota(jnp.int32, (tm, LANES), 1)
    ti = jnp.zeros((tm, LANES), jnp.int32)
    tg = jnp.zeros((tm, LANES), F32)
    for k in range(top_k):
        rank_k = jnp.sum(jnp.where(lane == firsts[k], before, 0.0), axis=-1, keepdims=True)
        ti = jnp.where(out_lane == k, firsts[k], ti)
        ti = jnp.where(out_lane == top_k + k, rank_k.astype(jnp.int32), ti)
        tg = jnp.where(out_lane == k, exps[k] / den, tg)
    ti_ref[...] = ti
    tg_ref[...] = tg
    cnt_ref[...] = cnt_ref[...] + jnp.sum(sel, axis=0, keepdims=True)


def moe_router(x, w_router, b_router):
    n, d = x.shape
    n_exp = w_router.shape[1]
    tm = min(ROUTER_TM, n)
    return pl.pallas_call(
        functools.partial(_router_kernel, n_exp=n_exp, top_k=TOP_K),
        grid=(n // tm,),
        in_specs=[pl.BlockSpec((tm, d), lambda i: (i, 0)),
                  pl.BlockSpec((d, n_exp), lambda i: (0, 0)),
                  pl.BlockSpec((1, n_exp), lambda i: (0, 0))],
        out_specs=[pl.BlockSpec((tm, LANES), lambda i: (i, 0)),
                   pl.BlockSpec((tm, LANES), lambda i: (i, 0)),
                   pl.BlockSpec((1, n_exp), lambda i: (0, 0))],
        out_shape=[jax.ShapeDtypeStruct((n, LANES), jnp.int32),
                   jax.ShapeDtypeStruct((n, LANES), F32),
                   jax.ShapeDtypeStruct((1, n_exp), F32)],
        compiler_params=_cparams(("arbitrary",)),
        name="moe_router",
    )(x, w_router, b_router.reshape(1, n_exp))


def _dispatch_kernel(dest_ref, pad_start_ref, pad_cnt_ref, x_hbm, xs_hbm, zero_ref, sem, zsem,
                     *, n_pairs, top_k, n_exp, max_pad):
    zero_ref[...] = jnp.zeros_like(zero_ref)

    def row_copy(p):
        return pltpu.make_async_copy(x_hbm.at[pl.ds(p // top_k, 1)],
                                     xs_hbm.at[pl.ds(dest_ref[p], 1)], sem.at[(p // DMA_CHUNK) % 2])

    n_chunks = n_pairs // DMA_CHUNK

    def chunk_body(c, carry):
        def issue(j, cc):
            row_copy(c * DMA_CHUNK + j).start()
            return cc
        lax.fori_loop(0, DMA_CHUNK, issue, 0)

        @pl.when(c > 0)
        def _():
            def drain(j, cc):
                row_copy((c - 1) * DMA_CHUNK + j).wait()
                return cc
            lax.fori_loop(0, DMA_CHUNK, drain, 0)
        return carry

    lax.fori_loop(0, n_chunks, chunk_body, 0)

    def drain_last(j, cc):
        row_copy((n_chunks - 1) * DMA_CHUNK + j).wait()
        return cc
    lax.fori_loop(0, DMA_CHUNK, drain_last, 0)

    def pad_copy(e, r):
        return pltpu.make_async_copy(zero_ref.at[pl.ds(0, 1)],
                                     xs_hbm.at[pl.ds(pad_start_ref[e] + r, 1)], zsem)

    def pad_body(e, carry):
        def issue(r, cc):
            pad_copy(e, r).start()
            return cc
        lax.fori_loop(0, pad_cnt_ref[e], issue, 0)

        def drain(r, cc):
            pad_copy(e, r).wait()
            return cc
        lax.fori_loop(0, pad_cnt_ref[e], drain, 0)
        return carry

    lax.fori_loop(0, n_exp, pad_body, 0)


def moe_dispatch(x, dest, pad_start, pad_cnt, n_rows):
    n, d = x.shape
    n_pairs = dest.shape[0]
    n_exp = pad_start.shape[0]
    return pl.pallas_call(
        functools.partial(_dispatch_kernel, n_pairs=n_pairs, top_k=TOP_K, n_exp=n_exp,
                          max_pad=MOE_SUB),
        grid_spec=pltpu.PrefetchScalarGridSpec(
            num_scalar_prefetch=3,
            grid=(1,),
            in_specs=[pl.BlockSpec(memory_space=pl.ANY)],
            out_specs=pl.BlockSpec(memory_space=pl.ANY),
            scratch_shapes=[pltpu.VMEM((8, d), F32),
                            pltpu.SemaphoreType.DMA((2,)),
                            pltpu.SemaphoreType.DMA(())],
        ),
        out_shape=jax.ShapeDtypeStruct((n_rows, d), F32),
        compiler_params=_cparams(("arbitrary",)),
        name="moe_dispatch",
    )(dest, pad_start, pad_cnt, x)


def _expert_kernel(blk_e_ref, blk_rows_ref, blk_src_ref,
                   x_ref, wg_ref, wu_ref, bg_ref, bu_ref, wd_ref, bd_ref, o_ref,
                   wg_s, wu_s, wd_s):
    s = pl.program_id(0)
    j = pl.program_id(1)
    rows = blk_rows_ref[s]

    @pl.when(rows > 0)
    def _():
        wg_s[...] = wg_ref[0].astype(BF16)
        wu_s[...] = wu_ref[0].astype(BF16)
        wd_s[...] = wd_ref[0].astype(BF16)
        for r in range(MOE_SUPER // MOE_SUB):
            @pl.when(r * MOE_SUB < rows)
            def _():
                sl = pl.ds(r * MOE_SUB, MOE_SUB)
                xb = x_ref[sl, :].astype(BF16)
                hg = jnp.dot(xb, wg_s[...], preferred_element_type=F32) + bg_ref[0]
                hu = jnp.dot(xb, wu_s[...], preferred_element_type=F32) + bu_ref[0]
                gl = jnp.minimum(hg, SWIGLU_LIMIT)
                up = jnp.clip(hu, -SWIGLU_LIMIT, SWIGLU_LIMIT)
                act = gl * jax.nn.sigmoid(SWIGLU_ALPHA * gl) * (up + 1.0)
                contrib = jnp.dot(act.astype(BF16), wd_s[...], preferred_element_type=F32)

                @pl.when(j == 0)
                def _():
                    o_ref[sl, :] = contrib + bd_ref[0]

                @pl.when(j > 0)
                def _():
                    o_ref[sl, :] = o_ref[sl, :] + contrib


def moe_experts(xs, blk_e, blk_rows, blk_src, w_gu, b_gu, w_dn, b_dn):
    n_rows, d = xs.shape
    n_exp, _, two_f = w_gu.shape
    d_ff = two_f // 2
    n_super = n_rows // MOE_SUPER
    n_f = d_ff // MOE_TF

    def jeff(s, j, rows):
        return jnp.where(rows[s] > 0, j, n_f - 1)

    return pl.pallas_call(
        _expert_kernel,
        grid_spec=pltpu.PrefetchScalarGridSpec(
            num_scalar_prefetch=3,
            grid=(n_super, n_f),
            in_specs=[
                pl.BlockSpec((MOE_SUPER, d), lambda s, j, be, br, bs: (bs[s], 0)),
                pl.BlockSpec((1, d, MOE_TF), lambda s, j, be, br, bs: (be[s], 0, jeff(s, j, br))),
                pl.BlockSpec((1, d, MOE_TF),
                             lambda s, j, be, br, bs: (be[s], 0, n_f + jeff(s, j, br))),
                pl.BlockSpec((1, 1, MOE_TF), lambda s, j, be, br, bs: (be[s], 0, jeff(s, j, br))),
                pl.BlockSpec((1, 1, MOE_TF),
                             lambda s, j, be, br, bs: (be[s], 0, n_f + jeff(s, j, br))),
                pl.BlockSpec((1, MOE_TF, d), lambda s, j, be, br, bs: (be[s], jeff(s, j, br), 0)),
                pl.BlockSpec((1, 1, d), lambda s, j, be, br, bs: (be[s], 0, 0)),
            ],
            out_specs=pl.BlockSpec((MOE_SUPER, d), lambda s, j, be, br, bs: (bs[s], 0)),
            scratch_shapes=[pltpu.VMEM((d, MOE_TF), BF16),
                            pltpu.VMEM((d, MOE_TF), BF16),
                            pltpu.VMEM((MOE_TF, d), BF16)],
        ),
        out_shape=jax.ShapeDtypeStruct((n_rows, d), F32),
        compiler_params=_cparams(("arbitrary", "arbitrary")),
        name="moe_experts",
    )(blk_e, blk_rows, blk_src, xs, w_gu, w_gu, b_gu.reshape(n_exp, 1, two_f),
      b_gu.reshape(n_exp, 1, two_f), w_dn, b_dn.reshape(n_exp, 1, d))


def _combine_kernel(dest_ref, x_ref, tg_ref, g_ref, b_ref, yb_hbm, o_ref, buf, sem,
                    *, top_k, alpha, tb):
    i = pl.program_id(0)
    n_blk = pl.num_programs(0)

    def row_copy(blk, slot, q):
        return pltpu.make_async_copy(
            yb_hbm.at[pl.ds(dest_ref[blk * tb * top_k + q], 1)],
            buf.at[slot, q % top_k, pl.ds(q // top_k, 1)], sem.at[slot])

    def issue_block(blk, slot):
        def body(q, c):
            row_copy(blk, slot, q).start()
            return c
        lax.fori_loop(0, tb * top_k, body, 0)

    @pl.when(i == 0)
    def _():
        issue_block(0, 0)

    @pl.when(i + 1 < n_blk)
    def _():
        issue_block(i + 1, (i + 1) % 2)

    slot = i % 2

    def drain(q, c):
        row_copy(i, slot, q).wait()
        return c
    lax.fori_loop(0, tb * top_k, drain, 0)

    tg = tg_ref[...]
    acc = alpha * x_ref[...]
    for k in range(top_k):
        acc = acc + tg[:, k:k + 1] * buf[slot, k]
    o_ref[...] = _ln_rows(acc, g_ref[...], b_ref[...])


def moe_combine_ln(x, yb, dest, tg, g, b, alpha):
    n, d = x.shape
    tb = COMBINE_TB
    return pl.pallas_call(
        functools.partial(_combine_kernel, top_k=TOP_K, alpha=alpha, tb=tb),
        grid_spec=pltpu.PrefetchScalarGridSpec(
            num_scalar_prefetch=1,
            grid=(n // tb,),
            in_specs=[pl.BlockSpec((tb, d), lambda i, dr: (i, 0)),
                      pl.BlockSpec((tb, LANES), lambda i, dr: (i, 0)),
                      pl.BlockSpec((1, d), lambda i, dr: (0, 0)),
                      pl.BlockSpec((1, d), lambda i, dr: (0, 0)),
                      pl.BlockSpec(memory_space=pl.ANY)],
            out_specs=pl.BlockSpec((tb, d), lambda i, dr: (i, 0)),
            scratch_shapes=[pltpu.VMEM((2, TOP_K, tb, d), F32),
                            pltpu.SemaphoreType.DMA((2,))],
        ),
        out_shape=jax.ShapeDtypeStruct((n, d), F32),
        compiler_params=_cparams(("arbitrary",)),
        name="moe_combine_ln",
    )(dest, x, tg, g.reshape(1, d), b.reshape(1, d), yb)


def moe_layer(x, w_router, b_router, w_gu, b_gu, w_dn, b_dn, ln_g, ln_b, alpha):
    n, d = x.shape
    n_exp = w_router.shape[1]
    m = n * TOP_K
    n_super = -(-m // MOE_SUPER) + n_exp
    n_rows = n_super * MOE_SUPER

    ti, tg, cnt = moe_router(x, w_router, b_router)
    counts = cnt[0].astype(jnp.int32)
    nsup = (counts + MOE_SUPER - 1) // MOE_SUPER
    sup_end = jnp.cumsum(nsup)
    sup_start = sup_end - nsup
    row_start = sup_start * MOE_SUPER
    top_i = ti[:, :TOP_K]
    rank = ti[:, TOP_K:2 * TOP_K]
    dest = (row_start[top_i] + rank).reshape(m)
    s_idx = jnp.arange(n_super, dtype=jnp.int32)
    n_used = sup_end[-1]
    used = s_idx < n_used
    src = jnp.where(used, s_idx, n_used - 1)
    blk_e = jnp.minimum(jnp.searchsorted(sup_end, src, side="right"), n_exp - 1).astype(jnp.int32)
    blk_rows = jnp.clip(counts[blk_e] - (src - sup_start[blk_e]) * MOE_SUPER, 0, MOE_SUPER)
    blk_rows = jnp.where(used, blk_rows, 0).astype(jnp.int32)
    pad_start = row_start + counts
    pad_cnt = (-counts) % MOE_SUB

    xs = moe_dispatch(x, dest, pad_start, pad_cnt, n_rows)
    yb = moe_experts(xs, blk_e, blk_rows, src.astype(jnp.int32), w_gu, b_gu, w_dn, b_dn)
    return moe_combine_ln(x, yb, dest, tg, ln_g, ln_b, alpha)


def _hgrn2_group(q, f, i, gate, s0, lb, norm_g):
    bsz, t, d_a = q.shape
    n_h, dk, dv = s0.shape[1:]
    q = jax.nn.silu(q)
    logf = jnp.log(lb + (1.0 - lb) * jax.nn.sigmoid(f))
    k = (1.0 - lb) * jax.nn.sigmoid(-f)
    tp = -(-t // CHUNK_A) * CHUNK_A
    pad = [(0, 0), (0, tp - t), (0, 0)]
    q, k, v, logf = [jnp.pad(a, pad) for a in (q, k, i, logf)]
    nc = tp // CHUNK_A
    blocks = lambda a, dd: a.reshape(bsz, nc, CHUNK_A, n_h, dd).transpose(1, 0, 3, 2, 4)
    mask = jnp.tril(jnp.ones((CHUNK_A, CHUNK_A), bool))

    def step(S, inp):
        qc, kc, vc, gc = inp
        b = jnp.cumsum(gc, axis=2)
        diff = b[:, :, :, None, :] - b[:, :, None, :, :]
        dec = jnp.exp(jnp.where(mask[:, :, None], diff, -jnp.inf))
        att = jnp.sum(qc[:, :, :, None, :] * kc[:, :, None, :, :] * dec, axis=-1)
        o = (jnp.einsum('bhts,bhsv->bhtv', att, vc)
             + jnp.einsum('bhtd,bhdv->bhtv', qc * jnp.exp(b), S))
        bl = b[:, :, -1:, :]
        S = (jnp.exp(bl[:, :, 0, :])[..., None] * S
             + jnp.einsum('bhsd,bhsv->bhdv', kc * jnp.exp(bl - b), vc))
        return S, o

    s_t, o = lax.scan(step, s0, (blocks(q, dk), blocks(k, dk), blocks(v, dv), blocks(logf, dk)))
    o = o.transpose(1, 0, 3, 2, 4).reshape(bsz, tp, n_h, dv)[:, :t]
    o = o * lax.rsqrt(jnp.mean(o * o, -1, keepdims=True) + LN_EPS) * norm_g
    return o.reshape(bsz, t, d_a) * jax.nn.silu(gate), s_t


def _rwkv7_group(p, shift_prev, s0, mu, w0, w2, a0, a2, g2, k_k, k_a, r_k, gn_g, gn_b, chunk):
    bsz, t, _ = p.shape
    d_b = w0.shape[0]
    n_h, n_b = s0.shape[1], s0.shape[2]
    wl, al_ = w2.shape[0], a2.shape[0]
    prev = jnp.concatenate([shift_prev[:, None], p[:, :-1]], axis=1)
    xm = p + (prev - p) * mu
    r, k, v, wd, ad, gd = jnp.split(
        xm, [d_b, 2 * d_b, 3 * d_b, 3 * d_b + wl, 3 * d_b + wl + al_], axis=-1)
    wlog = -jax.nn.softplus(-(w0 + jnp.tanh(wd) @ w2)) - 0.5
    logw = -jnp.exp(wlog)
    a = jax.nn.sigmoid(a0 + ad @ a2)
    g = jax.nn.sigmoid(gd) @ g2
    hs = lambda x: x.reshape(bsz, t, n_h, n_b)
    r, k, v, logw, a = map(hs, (r, k, v, logw, a))
    kk = k * k_k.reshape(n_h, n_b)
    kk = kk / jnp.maximum(jnp.sqrt(jnp.sum(kk * kk, -1, keepdims=True)), 1e-12)
    k2 = k * (1.0 + (a - 1.0) * k_a.reshape(n_h, n_b))
    al = -kk
    be = kk * a
    c = chunk
    nc = t // c
    ch = lambda x: x.reshape(bsz, nc, c, n_h, n_b).transpose(1, 0, 3, 2, 4)
    strict = jnp.tril(jnp.ones((c, c), bool), -1)
    incl = jnp.tril(jnp.ones((c, c), bool))
    ein = functools.partial(jnp.einsum, precision=HIGHEST)

    def step(S, inp):
        rc, kc, vc, lc, ac, bc = inp
        cl = jnp.cumsum(lc, axis=2)
        at = ac * jnp.exp(cl - lc)
        rt = rc * jnp.exp(cl)
        bt = bc * jnp.exp(-cl)
        kt = kc * jnp.exp(-cl)
        nm = jnp.where(strict, ein('bhck,bhik->bhci', at, bt), 0.0)
        aak = jnp.where(strict, ein('bhck,bhik->bhci', at, kt), 0.0)
        arb = jnp.where(incl, ein('bhck,bhik->bhci', rt, bt), 0.0)
        ark = jnp.where(incl, ein('bhck,bhik->bhci', rt, kt), 0.0)
        u = ein('bhck,bhvk->bhcv', at, S) + ein('bhci,bhiv->bhcv', aak, vc)
        pw = nm
        n = 1
        while n < c:
            u = u + ein('bhci,bhiv->bhcv', pw, u)
            n *= 2
            if n < c:
                pw = ein('bhci,bhij->bhcj', pw, pw)
        y = (ein('bhck,bhvk->bhcv', rt, S) + ein('bhci,bhiv->bhcv', arb, u)
             + ein('bhci,bhiv->bhcv', ark, vc))
        sn = jnp.exp(cl[:, :, -1])[:, :, None, :] * (
            S + ein('bhiv,bhik->bhvk', u, bt) + ein('bhiv,bhik->bhvk', vc, kt))
        return sn, y

    s_t, y = lax.scan(step, s0, tuple(map(ch, (r, k2, v, logw, al, be))))
    y = y.transpose(1, 0, 3, 2, 4).reshape(bsz, t, n_h, n_b)
    mean = jnp.mean(y, -1, keepdims=True)
    var = jnp.mean(jnp.square(y - mean), -1, keepdims=True)
    y = (y - mean) * lax.rsqrt(var + RWKV_GN_EPS) * gn_g.reshape(n_h, n_b) + gn_b.reshape(n_h, n_b)
    y = y + jnp.sum(r * k2 * r_k, -1, keepdims=True) * v
    return y.reshape(bsz, t, d_b) * g, s_t, p[:, -1]


def _pool_group(x, hist, start_pos, w_pool, scale):
    bsz, t, d = x.shape
    n_g = len(POOL_WINDOWS)
    pc = d // n_g
    hlen = hist.shape[1]
    xx = jnp.concatenate([hist, x], axis=1)
    pos = jnp.arange(t) + start_pos
    outs = []
    for gi, w in enumerate(POOL_WINDOWS):
        sl = slice(gi * pc, (gi + 1) * pc)
        acc = x[..., sl]
        for back in range(1, w):
            acc = acc + xx[:, hlen - back:hlen - back + t, sl]
        cnt = jnp.minimum(pos + 1, w).astype(F32)[None, :, None]
        outs.append(acc / cnt - x[..., sl])
    u = jnp.stack(outs, axis=2)
    y = jnp.einsum('btgc,gcd->btgd', u, w_pool).reshape(bsz, t, d) * scale
    return y, xx[:, -hlen:]


def kernel(x_prompt, x_sample, state_hgrn, state_rwkv, state_rwkv_shift, state_pool, mix_w_in, hgrn_lb, hgrn_norm_g, rwkv_mu, rwkv_w0, rwkv_w2, rwkv_a0, rwkv_a2, rwkv_g2, rwkv_k_k, rwkv_k_a, rwkv_r_k, rwkv_gn_g, rwkv_gn_b, mix_w_out, pool_w, pool_scale, ln1_g, ln1_b, ln2_g, ln2_b, moe_w_router, moe_b_router, moe_w_gu, moe_b_gu, moe_w_dn, moe_b_dn):
    bp, tp, d = x_prompt.shape
    bs, ts, _ = x_sample.shape
    depth = ln1_g.shape[0]
    n_h_a, dk_a, dv_a = state_hgrn.shape[2:]
    n_h_b, n_b = state_rwkv.shape[2:4]
    d_a = n_h_a * dk_a
    pb = state_rwkv_shift.shape[2]
    hist = state_pool.shape[2]
    n_p = bp * tp
    alpha = (2 * depth) ** 0.25

    lb_p = jax.nn.softmax(hgrn_lb, axis=0)
    lb_all = jnp.cumsum(lb_p, axis=0) - lb_p[0]
    x = jnp.concatenate([x_prompt.reshape(n_p, d), x_sample.reshape(bs * ts, d)], axis=0)
    hg_p, hg_s, rw_p, rw_s, sh_p, sh_s, pl_p, pl_s = [], [], [], [], [], [], [], []
    for l in range(depth):
        if l % 2 == 0:
            e = l // 2
            p = matmul(x, mix_w_in[e])
            rw = (rwkv_mu[e], rwkv_w0[e], rwkv_w2[e], rwkv_a0[e], rwkv_a2[e], rwkv_g2[e],
                  rwkv_k_k[e], rwkv_k_a[e], rwkv_r_k[e], rwkv_gn_g[e], rwkv_gn_b[e])
            outs = []
            for grp, (lo, hi, bsz, t) in enumerate(((0, n_p, bp, tp), (n_p, n_p + bs * ts, bs, ts))):
                pg = p[lo:hi].reshape(bsz, t, -1)
                if grp == 0:
                    s_h = jnp.zeros((bsz, n_h_a, dk_a, dv_a), F32)
                    s_r = jnp.zeros((bsz, n_h_b, n_b, n_b), F32)
                    s_s = jnp.zeros((bsz, pb), F32)
                else:
                    s_h, s_r, s_s = state_hgrn[e], state_rwkv[e], state_rwkv_shift[e]
                qa, fa, ia, ga = [pg[..., n * d_a:(n + 1) * d_a] for n in range(4)]
                o_a, n_h = _hgrn2_group(qa, fa, ia, ga, s_h, lb_all[e], hgrn_norm_g[e])
                o_b, n_r, n_s = _rwkv7_group(pg[..., 4 * d_a:], s_s, s_r, *rw, chunk=min(t, 64))
                outs.append((o_a.reshape(bsz * t, -1), o_b.reshape(bsz * t, -1)))
                (hg_p, hg_s)[grp].append(n_h)
                (rw_p, rw_s)[grp].append(n_r)
                (sh_p, sh_s)[grp].append(n_s)
            o_a = jnp.concatenate([outs[0][0], outs[1][0]], axis=0)
            o_b = jnp.concatenate([outs[0][1], outs[1][1]], axis=0)
            w_out = mix_w_out[e].astype(BF16)
            x = proj_residual_ln(x, [o_a, o_b], [w_out[:d_a], w_out[d_a:]], ln1_g[l], ln1_b[l], alpha)
        else:
            j = l // 2
            xp3 = x[:n_p].reshape(bp, tp, d)
            xs3 = x[n_p:].reshape(bs, ts, d)
            h_p, c1 = _pool_group(xp3, jnp.zeros((bp, hist, d), F32), 0, pool_w[j], pool_scale[j])
            h_s, c2 = _pool_group(xs3, state_pool[j], PAST_LEN, pool_w[j], pool_scale[j])
            pl_p.append(c1)
            pl_s.append(c2)
            h = jnp.concatenate([h_p.reshape(n_p, d), h_s.reshape(bs * ts, d)], axis=0)
            x = _ln_rows(alpha * x + h, ln1_g[l], ln1_b[l])
        x = moe_layer(x, moe_w_router[l], moe_b_router[l], moe_w_gu[l], moe_b_gu[l],
                      moe_w_dn[l], moe_b_dn[l], ln2_g[l], ln2_b[l], alpha)
    return (x[:n_p].reshape(bp, tp, d), x[n_p:].reshape(bs, ts, d),
            jnp.stack(hg_p), jnp.stack(hg_s), jnp.stack(rw_p), jnp.stack(rw_s),
            jnp.stack(sh_p), jnp.stack(sh_s), jnp.stack(pl_p), jnp.stack(pl_s))
```

```python
import functools

import jax
import jax.numpy as jnp
from jax import lax
from jax.experimental import pallas as pl
from jax.experimental.pallas import tpu as pltpu

F32 = jnp.float32
BF16 = jnp.bfloat16
HIGHEST = lax.Precision.HIGHEST

CHUNK_A = 16
POOL_WINDOWS = (2, 4, 8, 16)
TOP_K = 4
SWIGLU_LIMIT = 7.0
SWIGLU_ALPHA = 1.702
LN_EPS = 1e-5
RWKV_GN_EPS = 64e-5
PAST_LEN = 16384

VMEM_LIMIT_BYTES = 56 * 1024 * 1024
LANES = 128

MOE_SUPER = 1024
MOE_SUB = 256
MOE_TF = 256
ROUTER_TM = 512
COMBINE_TB = 128
RWKV_CHUNK = 64
DISPATCH_TB = 256
DMA_UNROLL = 8


def _cparams(sem):
    return pltpu.CompilerParams(dimension_semantics=sem, vmem_limit_bytes=VMEM_LIMIT_BYTES)


def _mm_kernel(x_ref, w_ref, o_ref):
    o_ref[...] = jnp.dot(x_ref[...].astype(BF16), w_ref[...].astype(BF16),
                         preferred_element_type=F32)


def matmul(x, w, tm=512, tn=1024):
    m, k = x.shape
    n = w.shape[1]
    tm = min(tm, m)
    tn = min(tn, n)
    return pl.pallas_call(
        _mm_kernel,
        grid=(pl.cdiv(n, tn), pl.cdiv(m, tm)),
        in_specs=[pl.BlockSpec((tm, k), lambda j, i: (i, 0)),
                  pl.BlockSpec((k, tn), lambda j, i: (0, j))],
        out_specs=pl.BlockSpec((tm, tn), lambda j, i: (i, j)),
        out_shape=jax.ShapeDtypeStruct((m, n), F32),
        compiler_params=_cparams(("arbitrary", "arbitrary")),
        name="matmul",
    )(x, w)


def _ln_rows(z, g, b):
    mu = jnp.mean(z, axis=-1, keepdims=True)
    zc = z - mu
    var = jnp.mean(zc * zc, axis=-1, keepdims=True)
    return zc * lax.rsqrt(var + LN_EPS) * g + b


def _proj_ln_kernel(*refs, n_in, alpha):
    x_ref = refs[0]
    a_refs = refs[1:1 + n_in]
    w_refs = refs[1 + n_in:1 + 2 * n_in]
    g_ref, b_ref, o_ref = refs[1 + 2 * n_in:]
    acc = alpha * x_ref[...]
    for a_ref, w_ref in zip(a_refs, w_refs):
        acc = acc + jnp.dot(a_ref[...].astype(BF16), w_ref[...], preferred_element_type=F32)
    o_ref[...] = _ln_rows(acc, g_ref[...], b_ref[...])


def proj_residual_ln(x, acts, weights, g, b, alpha, tm=256):
    m, d = x.shape
    n_in = len(acts)
    in_specs = [pl.BlockSpec((tm, d), lambda i: (i, 0))]
    in_specs += [pl.BlockSpec((tm, a.shape[1]), lambda i: (i, 0)) for a in acts]
    in_specs += [pl.BlockSpec(w.shape, lambda i: (0, 0)) for w in weights]
    in_specs += [pl.BlockSpec((1, d), lambda i: (0, 0))] * 2
    return pl.pallas_call(
        functools.partial(_proj_ln_kernel, n_in=n_in, alpha=alpha),
        grid=(m // tm,),
        in_specs=in_specs,
        out_specs=pl.BlockSpec((tm, d), lambda i: (i, 0)),
        out_shape=jax.ShapeDtypeStruct((m, d), F32),
        compiler_params=_cparams(("arbitrary",)),
        name="proj_residual_ln",
    )(x, *acts, *weights, g.reshape(1, d), b.reshape(1, d))


def _router_kernel(x_ref, w_ref, b_ref, ti_ref, tg_ref, cnt_ref, *, n_exp, top_k):
    i = pl.program_id(0)

    @pl.when(i == 0)
    def _():
        cnt_ref[...] = jnp.zeros_like(cnt_ref)

    logits = jnp.dot(x_ref[...].astype(BF16), w_ref[...].astype(BF16),
                     preferred_element_type=F32) + b_ref[...]
    tm = logits.shape[0]
    lane = lax.broadcasted_iota(jnp.int32, logits.shape, 1)
    work = logits
    firsts, vals, sel = [], [], jnp.zeros(logits.shape, F32)
    for _ in range(top_k):
        mx = jnp.max(work, axis=-1, keepdims=True)
        first = jnp.min(jnp.where(work == mx, lane, n_exp), axis=-1, keepdims=True)
        pick = lane == first
        firsts.append(first)
        vals.append(mx)
        sel = sel + pick.astype(F32)
        work = jnp.where(pick, -jnp.inf, work)
    exps = [jnp.exp(v - vals[0]) for v in vals]
    den = exps[0]
    for e in exps[1:]:
        den = den + e
    row = lax.broadcasted_iota(jnp.int32, (tm, tm), 0)
    col = lax.broadcasted_iota(jnp.int32, (tm, tm), 1)
    tril = jnp.where(row > col, 1.0, 0.0).astype(BF16)
    before = jnp.dot(tril, sel.astype(BF16), preferred_element_type=F32) + cnt_ref[...]
    out_lane = lax.broadcasted_iota(jnp.int32, (tm, LANES), 1)
    ti = jnp.zeros((tm, LANES), jnp.int32)
    tg = jnp.zeros((tm, LANES), F32)
    for k in range(top_k):
        rank_k = jnp.sum(jnp.where(lane == firsts[k], before, 0.0), axis=-1, keepdims=True)
        ti = jnp.where(out_lane == k, firsts[k], ti)
        ti = jnp.where(out_lane == top_k + k, rank_k.astype(jnp.int32), ti)
        tg = jnp.where(out_lane == k, exps[k] / den, tg)
    ti_ref[...] = ti
    tg_ref[...] = tg
    cnt_ref[...] = cnt_ref[...] + jnp.sum(sel, axis=0, keepdims=True)


def moe_router(x, w_router, b_router):
    n, d = x.shape
    n_exp = w_router.shape[1]
    tm = min(ROUTER_TM, n)
    return pl.pallas_call(
        functools.partial(_router_kernel, n_exp=n_exp, top_k=TOP_K),
        grid=(n // tm,),
        in_specs=[pl.BlockSpec((tm, d), lambda i: (i, 0)),
                  pl.BlockSpec((d, n_exp), lambda i: (0, 0)),
                  pl.BlockSpec((1, n_exp), lambda i: (0, 0))],
        out_specs=[pl.BlockSpec((tm, LANES), lambda i: (i, 0)),
                   pl.BlockSpec((tm, LANES), lambda i: (i, 0)),
                   pl.BlockSpec((1, n_exp), lambda i: (0, 0))],
        out_shape=[jax.ShapeDtypeStruct((n, LANES), jnp.int32),
                   jax.ShapeDtypeStruct((n, LANES), F32),
                   jax.ShapeDtypeStruct((1, n_exp), F32)],
        compiler_params=_cparams(("arbitrary",)),
        name="moe_router",
    )(x, w_router, b_router.reshape(1, n_exp))


def _dispatch_kernel(dest_ref, pad_start_ref, pad_cnt_ref, x_ref, xs_hbm, zero_ref, sem, zsem,
                     *, top_k, n_exp, tb):
    i = pl.program_id(0)
    base = i * (tb * top_k)

    def row_copy(r, k):
        return pltpu.make_async_copy(x_ref.at[pl.ds(r, 1)],
                                     xs_hbm.at[pl.ds(dest_ref[base + r * top_k + k], 1)], sem)

    def for_rows(fn):
        def body(g, c):
            for u in range(DMA_UNROLL):
                for k in range(top_k):
                    fn(row_copy(g * DMA_UNROLL + u, k))
            return c
        lax.fori_loop(0, tb // DMA_UNROLL, body, 0)

    for_rows(lambda cp: cp.start())

    @pl.when(i == 0)
    def _():
        zero_ref[...] = jnp.zeros_like(zero_ref)

        def pad_copy(e, r):
            return pltpu.make_async_copy(zero_ref.at[pl.ds(0, 1)],
                                         xs_hbm.at[pl.ds(pad_start_ref[e] + r, 1)], zsem)

        def pad_body(e, carry):
            def issue(r, cc):
                pad_copy(e, r).start()
                return cc
            lax.fori_loop(0, pad_cnt_ref[e], issue, 0)

            def drain(r, cc):
                pad_copy(e, r).wait()
                return cc
            lax.fori_loop(0, pad_cnt_ref[e], drain, 0)
            return carry

        lax.fori_loop(0, n_exp, pad_body, 0)

    for_rows(lambda cp: cp.wait())


def moe_dispatch(x, dest, pad_start, pad_cnt, n_rows):
    n, d = x.shape
    n_exp = pad_start.shape[0]
    tb = min(DISPATCH_TB, n)
    return pl.pallas_call(
        functools.partial(_dispatch_kernel, top_k=TOP_K, n_exp=n_exp, tb=tb),
        grid_spec=pltpu.PrefetchScalarGridSpec(
            num_scalar_prefetch=3,
            grid=(n // tb,),
            in_specs=[pl.BlockSpec((tb, d), lambda i, de, ps, pc: (i, 0))],
            out_specs=pl.BlockSpec(memory_space=pl.ANY),
            scratch_shapes=[pltpu.VMEM((8, d), F32),
                            pltpu.SemaphoreType.DMA(()),
                            pltpu.SemaphoreType.DMA(())],
        ),
        out_shape=jax.ShapeDtypeStruct((n_rows, d), F32),
        compiler_params=_cparams(("arbitrary",)),
        name="moe_dispatch",
    )(dest, pad_start, pad_cnt, x)


def _expert_kernel(blk_e_ref, blk_rows_ref, blk_src_ref,
                   x_ref, wg_ref, wu_ref, bg_ref, bu_ref, wd_ref, bd_ref, o_ref,
                   wg_s, wu_s, wd_s):
    s = pl.program_id(0)
    j = pl.program_id(1)
    rows = blk_rows_ref[s]

    @pl.when(rows > 0)
    def _():
        wg_s[...] = wg_ref[0].astype(BF16)
        wu_s[...] = wu_ref[0].astype(BF16)
        wd_s[...] = wd_ref[0].astype(BF16)
        for r in range(MOE_SUPER // MOE_SUB):
            @pl.when(r * MOE_SUB < rows)
            def _():
                sl = pl.ds(r * MOE_SUB, MOE_SUB)
                xb = x_ref[sl, :].astype(BF16)
                hg = jnp.dot(xb, wg_s[...], preferred_element_type=F32) + bg_ref[0]
                hu = jnp.dot(xb, wu_s[...], preferred_element_type=F32) + bu_ref[0]
                gl = jnp.minimum(hg, SWIGLU_LIMIT)
                up = jnp.clip(hu, -SWIGLU_LIMIT, SWIGLU_LIMIT)
                act = gl * jax.nn.sigmoid(SWIGLU_ALPHA * gl) * (up + 1.0)
                contrib = jnp.dot(act.astype(BF16), wd_s[...], preferred_element_type=F32)

                @pl.when(j == 0)
                def _():
                    o_ref[sl, :] = contrib + bd_ref[0]

                @pl.when(j > 0)
                def _():
                    o_ref[sl, :] = o_ref[sl, :] + contrib


def moe_experts(xs, blk_e, blk_rows, blk_src, w_gu, b_gu, w_dn, b_dn):
    n_rows, d = xs.shape
    n_exp, _, two_f = w_gu.shape
    d_ff = two_f // 2
    n_super = n_rows // MOE_SUPER
    n_f = d_ff // MOE_TF

    def jeff(s, j, rows):
        return jnp.where(rows[s] > 0, j, n_f - 1)

    return pl.pallas_call(
        _expert_kernel,
        grid_spec=pltpu.PrefetchScalarGridSpec(
            num_scalar_prefetch=3,
            grid=(n_super, n_f),
            in_specs=[
                pl.BlockSpec((MOE_SUPER, d), lambda s, j, be, br, bs: (bs[s], 0)),
                pl.BlockSpec((1, d, MOE_TF), lambda s, j, be, br, bs: (be[s], 0, jeff(s, j, br))),
                pl.BlockSpec((1, d, MOE_TF),
                             lambda s, j, be, br, bs: (be[s], 0, n_f + jeff(s, j, br))),
                pl.BlockSpec((1, 1, MOE_TF), lambda s, j, be, br, bs: (be[s], 0, jeff(s, j, br))),
                pl.BlockSpec((1, 1, MOE_TF),
                             lambda s, j, be, br, bs: (be[s], 0, n_f + jeff(s, j, br))),
                pl.BlockSpec((1, MOE_TF, d), lambda s, j, be, br, bs: (be[s], jeff(s, j, br), 0)),
                pl.BlockSpec((1, 1, d), lambda s, j, be, br, bs: (be[s], 0, 0)),
            ],
            out_specs=pl.BlockSpec((MOE_SUPER, d), lambda s, j, be, br, bs: (bs[s], 0)),
            scratch_shapes=[pltpu.VMEM((d, MOE_TF), BF16),
                            pltpu.VMEM((d, MOE_TF), BF16),
                            pltpu.VMEM((MOE_TF, d), BF16)],
        ),
        out_shape=jax.ShapeDtypeStruct((n_rows, d), F32),
        compiler_params=_cparams(("arbitrary", "arbitrary")),
        name="moe_experts",
    )(blk_e, blk_rows, blk_src, xs, w_gu, w_gu, b_gu.reshape(n_exp, 1, two_f),
      b_gu.reshape(n_exp, 1, two_f), w_dn, b_dn.reshape(n_exp, 1, d))


def _combine_kernel(dest_ref, x_ref, tg_ref, g_ref, b_ref, yb_hbm, o_ref, buf, sem,
                    *, top_k, alpha, tb):
    i = pl.program_id(0)
    n_blk = pl.num_programs(0)

    def for_rows(blk, slot, fn):
        base = blk * (tb * top_k)

        def body(g, c):
            for u in range(DMA_UNROLL):
                r = g * DMA_UNROLL + u
                for k in range(top_k):
                    fn(pltpu.make_async_copy(
                        yb_hbm.at[pl.ds(dest_ref[base + r * top_k + k], 1)],
                        buf.at[slot, k, pl.ds(r, 1)], sem.at[slot]))
            return c
        lax.fori_loop(0, tb // DMA_UNROLL, body, 0)

    @pl.when(i == 0)
    def _():
        for_rows(0, 0, lambda cp: cp.start())

    @pl.when(i + 1 < n_blk)
    def _():
        for_rows(i + 1, (i + 1) % 2, lambda cp: cp.start())

    slot = i % 2
    for_rows(i, slot, lambda cp: cp.wait())

    tg = tg_ref[...]
    acc = alpha * x_ref[...]
    for k in range(top_k):
        acc = acc + tg[:, k:k + 1] * buf[slot, k]
    o_ref[...] = _ln_rows(acc, g_ref[...], b_ref[...])


def moe_combine_ln(x, yb, dest, tg, g, b, alpha):
    n, d = x.shape
    tb = COMBINE_TB
    return pl.pallas_call(
        functools.partial(_combine_kernel, top_k=TOP_K, alpha=alpha, tb=tb),
        grid_spec=pltpu.PrefetchScalarGridSpec(
            num_scalar_prefetch=1,
            grid=(n // tb,),
            in_specs=[pl.BlockSpec((tb, d), lambda i, dr: (i, 0)),
                      pl.BlockSpec((tb, LANES), lambda i, dr: (i, 0)),
                      pl.BlockSpec((1, d), lambda i, dr: (0, 0)),
                      pl.BlockSpec((1, d), lambda i, dr: (0, 0)),
                      pl.BlockSpec(memory_space=pl.ANY)],
            out_specs=pl.BlockSpec((tb, d), lambda i, dr: (i, 0)),
            scratch_shapes=[pltpu.VMEM((2, TOP_K, tb, d), F32),
                            pltpu.SemaphoreType.DMA((2,))],
        ),
        out_shape=jax.ShapeDtypeStruct((n, d), F32),
        compiler_params=_cparams(("arbitrary",)),
        name="moe_combine_ln",
    )(dest, x, tg, g.reshape(1, d), b.reshape(1, d), yb)


def moe_layer(x, w_router, b_router, w_gu, b_gu, w_dn, b_dn, ln_g, ln_b, alpha):
    n, d = x.shape
    n_exp = w_router.shape[1]
    m = n * TOP_K
    n_super = -(-m // MOE_SUPER) + n_exp
    n_rows = n_super * MOE_SUPER

    ti, tg, cnt = moe_router(x, w_router, b_router)
    counts = cnt[0].astype(jnp.int32)
    nsup = (counts + MOE_SUPER - 1) // MOE_SUPER
    sup_end = jnp.cumsum(nsup)
    sup_start = sup_end - nsup
    row_start = sup_start * MOE_SUPER
    top_i = ti[:, :TOP_K]
    rank = ti[:, TOP_K:2 * TOP_K]
    dest = (row_start[top_i] + rank).reshape(m)
    s_idx = jnp.arange(n_super, dtype=jnp.int32)
    n_used = sup_end[-1]
    used = s_idx < n_used
    src = jnp.where(used, s_idx, n_used - 1)
    blk_e = jnp.minimum(jnp.searchsorted(sup_end, src, side="right"), n_exp - 1).astype(jnp.int32)
    blk_rows = jnp.clip(counts[blk_e] - (src - sup_start[blk_e]) * MOE_SUPER, 0, MOE_SUPER)
    blk_rows = jnp.where(used, blk_rows, 0).astype(jnp.int32)
    pad_start = row_start + counts
    pad_cnt = (-counts) % MOE_SUB

    xs = moe_dispatch(x, dest, pad_start, pad_cnt, n_rows)
    yb = moe_experts(xs, blk_e, blk_rows, src.astype(jnp.int32), w_gu, b_gu, w_dn, b_dn)
    return moe_combine_ln(x, yb, dest, tg, ln_g, ln_b, alpha)


def _head_sum_matrix(n_b, scale):
    row = lax.broadcasted_iota(jnp.int32, (LANES, LANES), 0)
    col = lax.broadcasted_iota(jnp.int32, (LANES, LANES), 1)
    return jnp.where(row // n_b == col // n_b, scale, 0.0).astype(F32)


def _per_head(x, mat):
    outs = [jnp.dot(x[:, s * LANES:(s + 1) * LANES], mat, precision=HIGHEST,
                    preferred_element_type=F32) for s in range(x.shape[1] // LANES)]
    return jnp.concatenate(outs, axis=1)


def _rwkv_prep_kernel(p_ref, prev_ref, mu_ref, w0_ref, w2_ref, a0_ref, a2_ref, g2_ref, kk_ref,
                      ka_ref, r_ref, k_ref, v_ref, lw_ref, al_ref, be_ref, g_ref,
                      *, d_b, wl, al, n_b):
    p = p_ref[...]
    xm = p + (prev_ref[...] - p) * mu_ref[...]
    r = xm[:, :d_b]
    k = xm[:, d_b:2 * d_b]
    v = xm[:, 2 * d_b:3 * d_b]
    wd = xm[:, 3 * d_b:3 * d_b + wl]
    ad = xm[:, 3 * d_b + wl:3 * d_b + wl + al]
    gd = xm[:, 3 * d_b + wl + al:]
    dotb = lambda a, w_ref: jnp.dot(a.astype(BF16), w_ref[...].astype(BF16),
                                    preferred_element_type=F32)
    wlog = -jax.nn.softplus(-(w0_ref[...] + dotb(jnp.tanh(wd), w2_ref))) - 0.5
    a = jax.nn.sigmoid(a0_ref[...] + dotb(ad, a2_ref))
    kk = k * kk_ref[...]
    norm = jnp.sqrt(_per_head(kk * kk, _head_sum_matrix(n_b, 1.0)))
    kk = kk / jnp.maximum(norm, 1e-12)
    r_ref[...] = r
    k_ref[...] = k * (1.0 + (a - 1.0) * ka_ref[...])
    v_ref[...] = v
    lw_ref[...] = -jnp.exp(wlog)
    al_ref[...] = -kk
    be_ref[...] = kk * a
    g_ref[...] = dotb(jax.nn.sigmoid(gd), g2_ref)


def rwkv_prep(p, prev, mu, w0, w2, a0, a2, g2, k_k, k_a, n_b, tm=256):
    n, pb = p.shape
    d_b = w0.shape[0]
    wl, al = w2.shape[0], a2.shape[0]
    row = lambda c: pl.BlockSpec((tm, c), lambda i: (i, 0))
    full = lambda a: pl.BlockSpec(a.shape, lambda i: (0, 0))
    vec = lambda a: a.reshape(1, -1)
    consts = [vec(mu), vec(w0), w2, vec(a0), a2, g2, vec(k_k), vec(k_a)]
    return pl.pallas_call(
        functools.partial(_rwkv_prep_kernel, d_b=d_b, wl=wl, al=al, n_b=n_b),
        grid=(n // tm,),
        in_specs=[row(pb), row(pb)] + [full(c) for c in consts],
        out_specs=[row(d_b)] * 7,
        out_shape=[jax.ShapeDtypeStruct((n, d_b), F32)] * 7,
        compiler_params=_cparams(("arbitrary",)),
        name="rwkv_prep",
    )(p, prev, *consts)


def _rwkv_chunk_kernel(r_ref, k_ref, v_ref, lw_ref, al_ref, be_ref, g_ref, s0_ref, gng_ref,
                       gnb_ref, rk_ref, o_ref, s_out_ref, st_ref, *, n_b, chunk, n_pairs):
    c = pl.program_id(1)
    c2 = 2 * chunk

    @pl.when(c == 0)
    def _():
        st_ref[...] = s0_ref[0]

    dot_f32 = functools.partial(jnp.dot, precision=HIGHEST, preferred_element_type=F32)
    dot = lambda a, b: jnp.dot(a.astype(BF16), b.astype(BF16), preferred_element_type=F32)
    dot_nt = lambda a, b: lax.dot_general(a.astype(BF16), b.astype(BF16), (((1,), (1,)), ((), ())),
                                          preferred_element_type=F32)
    dot_tn = lambda a, b: lax.dot_general(a.astype(BF16), b.astype(BF16), (((0,), (0,)), ((), ())),
                                          preferred_element_type=F32)
    lane = lax.broadcasted_iota(jnp.int32, (chunk, LANES), 1)
    head_a = lane < n_b
    stack = lambda x: jnp.concatenate([jnp.where(head_a, x, 0.0), jnp.where(head_a, 0.0, x)], axis=0)
    twice = lambda x: jnp.concatenate([x, x], axis=0)
    row2 = lax.broadcasted_iota(jnp.int32, (c2, c2), 0)
    col2 = lax.broadcasted_iota(jnp.int32, (c2, c2), 1)
    same = (row2 // chunk) == (col2 // chunk)
    strict = same & (row2 > col2)
    incl = same & (row2 >= col2)
    rowc = lax.broadcasted_iota(jnp.int32, (chunk, chunk), 0)
    colc = lax.broadcasted_iota(jnp.int32, (chunk, chunk), 1)
    cum = jnp.where(rowc >= colc, 1.0, 0.0).astype(F32)
    rl = lax.broadcasted_iota(jnp.int32, (LANES, LANES), 0)
    cl_ = lax.broadcasted_iota(jnp.int32, (LANES, LANES), 1)
    blockdiag = (rl // n_b) == (cl_ // n_b)
    eye = rl == cl_
    mean_mat = _head_sum_matrix(n_b, 1.0 / n_b)
    sum_mat = _head_sum_matrix(n_b, 1.0)

    for p in range(n_pairs):
        sl = slice(p * LANES, (p + 1) * LANES)
        r, k, v = r_ref[:, sl], k_ref[:, sl], v_ref[:, sl]
        lw, al, be = lw_ref[:, sl], al_ref[:, sl], be_ref[:, sl]
        cl = dot_f32(cum, lw)
        e_neg = jnp.exp(-cl)
        at = al * jnp.exp(cl - lw)
        rt = r * jnp.exp(cl)
        bt = be * e_neg
        kt = k * e_neg
        la, lr = stack(at), stack(rt)
        rb, rk2 = twice(bt), twice(kt)
        nm = jnp.where(strict, dot_nt(la, rb), 0.0)
        aak = jnp.where(strict, dot_nt(la, rk2), 0.0)
        arb = jnp.where(incl, dot_nt(lr, rb), 0.0)
        ark = jnp.where(incl, dot_nt(lr, rk2), 0.0)
        m = st_ref[p]
        vs = stack(v)
        u = stack(dot(at, m)) + dot(aak, vs)
        pw = nm
        n = 1
        while n < chunk:
            u = u + dot(pw, u)
            n *= 2
            if n < chunk:
                pw = dot(pw, pw)
        ys = stack(dot(rt, m)) + dot(arb, u) + dot(ark, vs)
        y = ys[:chunk] + ys[chunk:]
        up = u[:chunk] + u[chunk:]
        pc = jnp.exp(cl[chunk - 1:chunk, :])
        pc_col = jnp.transpose(jnp.broadcast_to(pc, (8, LANES)))[:, 0:1]
        upd = dot_tn(be * jnp.exp(cl[chunk - 1:chunk, :] - cl), up) + \
            dot_tn(k * jnp.exp(cl[chunk - 1:chunk, :] - cl), v)
        st_ref[p] = pc_col * m + jnp.where(blockdiag, upd, 0.0)

        mean = dot_f32(y, mean_mat)
        yc = y - mean
        var = dot_f32(yc * yc, mean_mat)
        yn = yc * lax.rsqrt(var + RWKV_GN_EPS) * gng_ref[:, sl] + gnb_ref[:, sl]
        bonus = dot_f32(r * k * rk_ref[:, sl], sum_mat) * v
        o_ref[:, sl] = (yn + bonus) * g_ref[:, sl]

    @pl.when(c == pl.num_programs(1) - 1)
    def _():
        s_out_ref[0] = st_ref[...]


def rwkv_chunked(r, k, v, lw, al, be, g, s0_bd, gn_g, gn_b, r_k, bsz, t, n_b, chunk):
    n, d_b = r.shape
    n_pairs = d_b // LANES
    n_c = t // chunk
    row = pl.BlockSpec((chunk, d_b), lambda b, c: (b * n_c + c, 0))
    vec = pl.BlockSpec((1, d_b), lambda b, c: (0, 0))
    st = pl.BlockSpec((1, n_pairs, LANES, LANES), lambda b, c: (b, 0, 0, 0))
    return pl.pallas_call(
        functools.partial(_rwkv_chunk_kernel, n_b=n_b, chunk=chunk, n_pairs=n_pairs),
        grid=(bsz, n_c),
        in_specs=[row] * 7 + [st, vec, vec, vec],
        out_specs=[row, st],
        out_shape=[jax.ShapeDtypeStruct((n, d_b), F32),
                   jax.ShapeDtypeStruct(s0_bd.shape, F32)],
        scratch_shapes=[pltpu.VMEM((n_pairs, LANES, LANES), F32)],
        compiler_params=_cparams(("arbitrary", "arbitrary")),
        name="rwkv_chunked",
    )(r, k, v, lw, al, be, g, s0_bd, gn_g.reshape(1, d_b), gn_b.reshape(1, d_b),
      r_k.reshape(1, d_b))


def _state_to_blockdiag(s):
    bsz, n_h, n_v, n_k = s.shape
    st = jnp.swapaxes(s, 2, 3).reshape(bsz, n_h // 2, 2, n_k, n_v)
    z = jnp.zeros_like(st[:, :, 0])
    top = jnp.concatenate([st[:, :, 0], z], axis=-1)
    bot = jnp.concatenate([z, st[:, :, 1]], axis=-1)
    return jnp.concatenate([top, bot], axis=-2)


def _blockdiag_to_state(m, n_b):
    bsz, n_pairs = m.shape[:2]
    a = m[:, :, :n_b, :n_b]
    b = m[:, :, n_b:, n_b:]
    st = jnp.stack([a, b], axis=2).reshape(bsz, n_pairs * 2, n_b, n_b)
    return jnp.swapaxes(st, 2, 3)


def rwkv7_group(p, shift_prev, s0, mu, w0, w2, a0, a2, g2, k_k, k_a, r_k, gn_g, gn_b, chunk):
    bsz, t, pb = p.shape
    n_b = s0.shape[2]
    prev = jnp.concatenate([shift_prev[:, None], p[:, :-1]], axis=1).reshape(bsz * t, pb)
    tm = 256 if (bsz * t) % 256 == 0 else bsz * t
    r, k, v, lw, al, be, g = rwkv_prep(p.reshape(bsz * t, pb), prev, mu, w0, w2, a0, a2, g2,
                                       k_k, k_a, n_b, tm=tm)
    o, m = rwkv_chunked(r, k, v, lw, al, be, g, _state_to_blockdiag(s0), gn_g, gn_b, r_k,
                        bsz, t, n_b, chunk)
    return o, _blockdiag_to_state(m, n_b), p[:, -1]


def _hgrn_kernel(q_ref, f_ref, i_ref, gate_ref, lb_ref, ng_ref, s0_ref, o_ref, s_out_ref, st_ref,
                 *, n_h, sub, tb):
    c = pl.program_id(1)

    @pl.when(c == 0)
    def _():
        st_ref[...] = s0_ref[0]

    row = lax.broadcasted_iota(jnp.int32, (sub, sub), 0)
    col = lax.broadcasted_iota(jnp.int32, (sub, sub), 1)
    cum = jnp.where(row >= col, 1.0, 0.0).astype(F32)
    t_idx = lax.broadcasted_iota(jnp.int32, (sub, LANES), 0)
    bf = lambda x: x.astype(BF16)

    for h in range(n_h):
        sl = slice(h * LANES, (h + 1) * LANES)
        lb = lb_ref[:, sl]

        def body(j, s_mat):
            rows = pl.ds(pl.multiple_of(j * sub, sub), sub)
            q = jax.nn.silu(q_ref[rows, sl])
            z = f_ref[rows, sl]
            logf = jnp.log(lb + (1.0 - lb) * jax.nn.sigmoid(z))
            kk = (1.0 - lb) * jax.nn.sigmoid(-z)
            v = i_ref[rows, sl]
            b = jnp.dot(cum, logf, precision=HIGHEST, preferred_element_type=F32)
            o = jnp.dot(bf(q * jnp.exp(b)), bf(s_mat), preferred_element_type=F32)
            vb = bf(v).astype(F32)
            for s in range(sub):
                dec = jnp.exp(jnp.where(t_idx >= s, b - b[s:s + 1, :], -jnp.inf))
                att = jnp.sum(q * kk[s:s + 1, :] * dec, axis=-1, keepdims=True)
                o = o + bf(att).astype(F32) * vb[s:s + 1, :]
            bl = b[sub - 1:sub, :]
            kd = bf(kk * jnp.exp(bl - b))
            upd = lax.dot_general(kd, bf(v), (((0,), (0,)), ((), ())), preferred_element_type=F32)
            decay = jnp.exp(bl)
            s_new = jnp.transpose(jnp.broadcast_to(decay, (8, LANES)))[:, 0:1] * s_mat + upd
            o = o * lax.rsqrt(jnp.mean(o * o, axis=-1, keepdims=True) + LN_EPS) * ng_ref[...]
            o_ref[rows, sl] = o * jax.nn.silu(gate_ref[rows, sl])
            return s_new

        st_ref[h] = lax.fori_loop(0, tb // sub, body, st_ref[h])

    @pl.when(c == pl.num_programs(1) - 1)
    def _():
        s_out_ref[0] = st_ref[...]


def hgrn2(proj, s0, lb, norm_g, bsz, t):
    n = proj.shape[0]
    n_h, dk, dv = s0.shape[1:]
    d_a = n_h * dk
    sub = min(CHUNK_A, t)
    tb = min(256, t)
    n_c = t // tb
    col = lambda which: pl.BlockSpec((tb, d_a), lambda b, c: (b * n_c + c, which))
    st = pl.BlockSpec((1, n_h, dk, dv), lambda b, c: (b, 0, 0, 0))
    return pl.pallas_call(
        functools.partial(_hgrn_kernel, n_h=n_h, sub=sub, tb=tb),
        grid=(bsz, n_c),
        in_specs=[col(0), col(1), col(2), col(3),
                  pl.BlockSpec((1, d_a), lambda b, c: (0, 0)),
                  pl.BlockSpec((1, dv), lambda b, c: (0, 0)), st],
        out_specs=[pl.BlockSpec((tb, d_a), lambda b, c: (b * n_c + c, 0)), st],
        out_shape=[jax.ShapeDtypeStruct((n, d_a), F32), jax.ShapeDtypeStruct(s0.shape, F32)],
        scratch_shapes=[pltpu.VMEM((n_h, dk, dv), F32)],
        compiler_params=_cparams(("arbitrary", "arbitrary")),
        name="hgrn2",
    )(proj, proj, proj, proj, lb.reshape(1, d_a), norm_g.reshape(1, dv), s0)


def _pool_ln_kernel(x_ref, halo_ref, w_ref, sc_ref, g_ref, b_ref, o_ref, xx_ref,
                    *, windows, start_pos, alpha, tb, halo):
    c = pl.program_id(1)
    d = x_ref.shape[1]
    pc = d // len(windows)
    xx_ref[0:halo, :] = halo_ref[0, 0]
    xx_ref[halo:halo + tb, :] = x_ref[...]
    x = x_ref[...]
    pos = start_pos + c * tb + lax.broadcasted_iota(jnp.int32, (tb, 1), 0)
    ys = []
    for gi, w in enumerate(windows):
        sl = slice(gi * pc, (gi + 1) * pc)
        acc = x[:, sl]
        for back in range(1, w):
            acc = acc + xx_ref[halo - back:halo - back + tb, sl]
        cnt = jnp.minimum(pos + 1, w).astype(F32)
        u = acc / cnt - x[:, sl]
        ys.append(jnp.dot(u.astype(BF16), w_ref[gi], preferred_element_type=F32))
    y = jnp.concatenate(ys, axis=1) * sc_ref[...]
    o_ref[...] = _ln_rows(alpha * x + y, g_ref[...], b_ref[...])


def pool_residual_ln(x, hist, start_pos, w_pool, scale, g, b, alpha, bsz, t):
    n, d = x.shape
    halo = 16
    tb = min(256, t)
    n_c = t // tb
    x4 = x.reshape(bsz, n_c, tb, d)
    first = jnp.concatenate([jnp.zeros((bsz, halo - hist.shape[1], d), F32), hist], axis=1)
    halos = first[:, None]
    if n_c > 1:
        halos = jnp.concatenate([halos, x4[:, :-1, tb - halo:]], axis=1)
    vec = lambda a: pl.BlockSpec((1, d), lambda bi, c: (0, 0))
    return pl.pallas_call(
        functools.partial(_pool_ln_kernel, windows=POOL_WINDOWS, start_pos=start_pos,
                          alpha=alpha, tb=tb, halo=halo),
        grid=(bsz, n_c),
        in_specs=[pl.BlockSpec((tb, d), lambda bi, c: (bi * n_c + c, 0)),
                  pl.BlockSpec((1, 1, halo, d), lambda bi, c: (bi, c, 0, 0)),
                  pl.BlockSpec(w_pool.shape, lambda bi, c: (0, 0, 0)),
                  vec(scale), vec(g), vec(b)],
        out_specs=pl.BlockSpec((tb, d), lambda bi, c: (bi * n_c + c, 0)),
        out_shape=jax.ShapeDtypeStruct((n, d), F32),
        scratch_shapes=[pltpu.VMEM((halo + tb, d), F32)],
        compiler_params=_cparams(("arbitrary", "arbitrary")),
        name="pool_residual_ln",
    )(x, halos, w_pool.astype(BF16), scale.reshape(1, d), g.reshape(1, d), b.reshape(1, d))


def _hgrn2_group(q, f, i, gate, s0, lb, norm_g):
    bsz, t, d_a = q.shape
    n_h, dk, dv = s0.shape[1:]
    q = jax.nn.silu(q)
    logf = jnp.log(lb + (1.0 - lb) * jax.nn.sigmoid(f))
    k = (1.0 - lb) * jax.nn.sigmoid(-f)
    tp = -(-t // CHUNK_A) * CHUNK_A
    pad = [(0, 0), (0, tp - t), (0, 0)]
    q, k, v, logf = [jnp.pad(a, pad) for a in (q, k, i, logf)]
    nc = tp // CHUNK_A
    blocks = lambda a, dd: a.reshape(bsz, nc, CHUNK_A, n_h, dd).transpose(1, 0, 3, 2, 4)
    mask = jnp.tril(jnp.ones((CHUNK_A, CHUNK_A), bool))

    def step(S, inp):
        qc, kc, vc, gc = inp
        b = jnp.cumsum(gc, axis=2)
        diff = b[:, :, :, None, :] - b[:, :, None, :, :]
        dec = jnp.exp(jnp.where(mask[:, :, None], diff, -jnp.inf))
        att = jnp.sum(qc[:, :, :, None, :] * kc[:, :, None, :, :] * dec, axis=-1)
        o = (jnp.einsum('bhts,bhsv->bhtv', att, vc)
             + jnp.einsum('bhtd,bhdv->bhtv', qc * jnp.exp(b), S))
        bl = b[:, :, -1:, :]
        S = (jnp.exp(bl[:, :, 0, :])[..., None] * S
             + jnp.einsum('bhsd,bhsv->bhdv', kc * jnp.exp(bl - b), vc))
        return S, o

    s_t, o = lax.scan(step, s0, (blocks(q, dk), blocks(k, dk), blocks(v, dv), blocks(logf, dk)))
    o = o.transpose(1, 0, 3, 2, 4).reshape(bsz, tp, n_h, dv)[:, :t]
    o = o * lax.rsqrt(jnp.mean(o * o, -1, keepdims=True) + LN_EPS) * norm_g
    return o.reshape(bsz, t, d_a) * jax.nn.silu(gate), s_t


def _rwkv7_group(p, shift_prev, s0, mu, w0, w2, a0, a2, g2, k_k, k_a, r_k, gn_g, gn_b, chunk):
    bsz, t, _ = p.shape
    d_b = w0.shape[0]
    n_h, n_b = s0.shape[1], s0.shape[2]
    wl, al_ = w2.shape[0], a2.shape[0]
    prev = jnp.concatenate([shift_prev[:, None], p[:, :-1]], axis=1)
    xm = p + (prev - p) * mu
    r, k, v, wd, ad, gd = jnp.split(
        xm, [d_b, 2 * d_b, 3 * d_b, 3 * d_b + wl, 3 * d_b + wl + al_], axis=-1)
    wlog = -jax.nn.softplus(-(w0 + jnp.tanh(wd) @ w2)) - 0.5
    logw = -jnp.exp(wlog)
    a = jax.nn.sigmoid(a0 + ad @ a2)
    g = jax.nn.sigmoid(gd) @ g2
    hs = lambda x: x.reshape(bsz, t, n_h, n_b)
    r, k, v, logw, a = map(hs, (r, k, v, logw, a))
    kk = k * k_k.reshape(n_h, n_b)
    kk = kk / jnp.maximum(jnp.sqrt(jnp.sum(kk * kk, -1, keepdims=True)), 1e-12)
    k2 = k * (1.0 + (a - 1.0) * k_a.reshape(n_h, n_b))
    al = -kk
    be = kk * a
    c = chunk
    nc = t // c
    ch = lambda x: x.reshape(bsz, nc, c, n_h, n_b).transpose(1, 0, 3, 2, 4)
    strict = jnp.tril(jnp.ones((c, c), bool), -1)
    incl = jnp.tril(jnp.ones((c, c), bool))
    ein = functools.partial(jnp.einsum, precision=HIGHEST)

    def step(S, inp):
        rc, kc, vc, lc, ac, bc = inp
        cl = jnp.cumsum(lc, axis=2)
        at = ac * jnp.exp(cl - lc)
        rt = rc * jnp.exp(cl)
        bt = bc * jnp.exp(-cl)
        kt = kc * jnp.exp(-cl)
        nm = jnp.where(strict, ein('bhck,bhik->bhci', at, bt), 0.0)
        aak = jnp.where(strict, ein('bhck,bhik->bhci', at, kt), 0.0)
        arb = jnp.where(incl, ein('bhck,bhik->bhci', rt, bt), 0.0)
        ark = jnp.where(incl, ein('bhck,bhik->bhci', rt, kt), 0.0)
        u = ein('bhck,bhvk->bhcv', at, S) + ein('bhci,bhiv->bhcv', aak, vc)
        pw = nm
        n = 1
        while n < c:
            u = u + ein('bhci,bhiv->bhcv', pw, u)
            n *= 2
            if n < c:
                pw = ein('bhci,bhij->bhcj', pw, pw)
        y = (ein('bhck,bhvk->bhcv', rt, S) + ein('bhci,bhiv->bhcv', arb, u)
             + ein('bhci,bhiv->bhcv', ark, vc))
        sn = jnp.exp(cl[:, :, -1])[:, :, None, :] * (
            S + ein('bhiv,bhik->bhvk', u, bt) + ein('bhiv,bhik->bhvk', vc, kt))
        return sn, y

    s_t, y = lax.scan(step, s0, tuple(map(ch, (r, k2, v, logw, al, be))))
    y = y.transpose(1, 0, 3, 2, 4).reshape(bsz, t, n_h, n_b)
    mean = jnp.mean(y, -1, keepdims=True)
    var = jnp.mean(jnp.square(y - mean), -1, keepdims=True)
    y = (y - mean) * lax.rsqrt(var + RWKV_GN_EPS) * gn_g.reshape(n_h, n_b) + gn_b.reshape(n_h, n_b)
    y = y + jnp.sum(r * k2 * r_k, -1, keepdims=True) * v
    return y.reshape(bsz, t, d_b) * g, s_t, p[:, -1]


def _pool_group(x, hist, start_pos, w_pool, scale):
    bsz, t, d = x.shape
    n_g = len(POOL_WINDOWS)
    pc = d // n_g
    hlen = hist.shape[1]
    xx = jnp.concatenate([hist, x], axis=1)
    pos = jnp.arange(t) + start_pos
    outs = []
    for gi, w in enumerate(POOL_WINDOWS):
        sl = slice(gi * pc, (gi + 1) * pc)
        acc = x[..., sl]
        for back in range(1, w):
            acc = acc + xx[:, hlen - back:hlen - back + t, sl]
        cnt = jnp.minimum(pos + 1, w).astype(F32)[None, :, None]
        outs.append(acc / cnt - x[..., sl])
    u = jnp.stack(outs, axis=2)
    y = jnp.einsum('btgc,gcd->btgd', u, w_pool).reshape(bsz, t, d) * scale
    return y, xx[:, -hlen:]


def kernel(x_prompt, x_sample, state_hgrn, state_rwkv, state_rwkv_shift, state_pool, mix_w_in, hgrn_lb, hgrn_norm_g, rwkv_mu, rwkv_w0, rwkv_w2, rwkv_a0, rwkv_a2, rwkv_g2, rwkv_k_k, rwkv_k_a, rwkv_r_k, rwkv_gn_g, rwkv_gn_b, mix_w_out, pool_w, pool_scale, ln1_g, ln1_b, ln2_g, ln2_b, moe_w_router, moe_b_router, moe_w_gu, moe_b_gu, moe_w_dn, moe_b_dn):
    bp, tp, d = x_prompt.shape
    bs, ts, _ = x_sample.shape
    depth = ln1_g.shape[0]
    n_h_a, dk_a, dv_a = state_hgrn.shape[2:]
    n_h_b, n_b = state_rwkv.shape[2:4]
    d_a = n_h_a * dk_a
    pb = state_rwkv_shift.shape[2]
    hist = state_pool.shape[2]
    n_p = bp * tp
    alpha = (2 * depth) ** 0.25

    lb_p = jax.nn.softmax(hgrn_lb, axis=0)
    lb_all = jnp.cumsum(lb_p, axis=0) - lb_p[0]
    x = jnp.concatenate([x_prompt.reshape(n_p, d), x_sample.reshape(bs * ts, d)], axis=0)
    hg_p, hg_s, rw_p, rw_s, sh_p, sh_s, pl_p, pl_s = [], [], [], [], [], [], [], []
    for l in range(depth):
        if l % 2 == 0:
            e = l // 2
            p_a = matmul(x, mix_w_in[e][:, :4 * d_a])
            p_b = matmul(x, mix_w_in[e][:, 4 * d_a:])
            rw = (rwkv_mu[e], rwkv_w0[e], rwkv_w2[e], rwkv_a0[e], rwkv_a2[e], rwkv_g2[e],
                  rwkv_k_k[e], rwkv_k_a[e], rwkv_r_k[e], rwkv_gn_g[e], rwkv_gn_b[e])
            outs = []
            for grp, (lo, hi, bsz, t) in enumerate(((0, n_p, bp, tp), (n_p, n_p + bs * ts, bs, ts))):
                if grp == 0:
                    s_h = jnp.zeros((bsz, n_h_a, dk_a, dv_a), F32)
                    s_r = jnp.zeros((bsz, n_h_b, n_b, n_b), F32)
                    s_s = jnp.zeros((bsz, pb), F32)
                else:
                    s_h, s_r, s_s = state_hgrn[e], state_rwkv[e], state_rwkv_shift[e]
                o_a, n_h = hgrn2(p_a[lo:hi], s_h, lb_all[e], hgrn_norm_g[e], bsz, t)
                o_b, n_r, n_s = rwkv7_group(p_b[lo:hi].reshape(bsz, t, pb), s_s, s_r, *rw,
                                            chunk=min(t, RWKV_CHUNK))
                outs.append((o_a, o_b))
                (hg_p, hg_s)[grp].append(n_h)
                (rw_p, rw_s)[grp].append(n_r)
                (sh_p, sh_s)[grp].append(n_s)
            o_a = jnp.concatenate([outs[0][0], outs[1][0]], axis=0)
            o_b = jnp.concatenate([outs[0][1], outs[1][1]], axis=0)
            w_out = mix_w_out[e].astype(BF16)
            x = proj_residual_ln(x, [o_a, o_b], [w_out[:d_a], w_out[d_a:]], ln1_g[l], ln1_b[l], alpha)
        else:
            j = l // 2
            x_p, x_s = x[:n_p], x[n_p:]
            hist_p = jnp.zeros((bp, hist, d), F32)
            pl_p.append(jnp.concatenate([hist_p, x_p.reshape(bp, tp, d)], axis=1)[:, -hist:])
            pl_s.append(jnp.concatenate([state_pool[j], x_s.reshape(bs, ts, d)], axis=1)[:, -hist:])
            pool = (pool_w[j], pool_scale[j], ln1_g[l], ln1_b[l], alpha)
            x = jnp.concatenate([pool_residual_ln(x_p, hist_p, 0, *pool, bp, tp),
                                 pool_residual_ln(x_s, state_pool[j], PAST_LEN, *pool, bs, ts)], axis=0)
        x = moe_layer(x, moe_w_router[l], moe_b_router[l], moe_w_gu[l], moe_b_gu[l],
                      moe_w_dn[l], moe_b_dn[l], ln2_g[l], ln2_b[l], alpha)
    return (x[:n_p].reshape(bp, tp, d), x[n_p:].reshape(bs, ts, d),
            jnp.stack(hg_p), jnp.stack(hg_s), jnp.stack(rw_p), jnp.stack(rw_s),
            jnp.stack(sh_p), jnp.stack(sh_s), jnp.stack(pl_p), jnp.stack(pl_s))
```

```python
import functools

import jax
import jax.numpy as jnp
from jax import lax
from jax.experimental import pallas as pl
from jax.experimental.pallas import tpu as pltpu

F32 = jnp.float32
BF16 = jnp.bfloat16
HIGHEST = lax.Precision.HIGHEST

CHUNK_A = 16
POOL_WINDOWS = (2, 4, 8, 16)
TOP_K = 4
SWIGLU_LIMIT = 7.0
SWIGLU_ALPHA = 1.702
LN_EPS = 1e-5
RWKV_GN_EPS = 64e-5
PAST_LEN = 16384

VMEM_LIMIT_BYTES = 56 * 1024 * 1024
LANES = 128

MOE_SUPER = 1024
MOE_SUB = 256
MOE_TF = 256
MOE_TN = 256
ROUTER_TM = 512
COMBINE_TB = 128
RWKV_CHUNK = 64
DISPATCH_TB = 256
DMA_UNROLL = 8


def _cparams(sem):
    return pltpu.CompilerParams(dimension_semantics=sem, vmem_limit_bytes=VMEM_LIMIT_BYTES)


def _mm_kernel(x_ref, w_ref, o_ref):
    o_ref[...] = jnp.dot(x_ref[...].astype(BF16), w_ref[0].astype(BF16),
                         preferred_element_type=F32)


def matmul(x, w, layer, col0, n, tm=512, tn=1024):
    m, k = x.shape
    tm = min(tm, m)
    assert col0 % tn == 0
    c0 = col0 // tn
    return pl.pallas_call(
        _mm_kernel,
        grid=(pl.cdiv(n, tn), pl.cdiv(m, tm)),
        in_specs=[pl.BlockSpec((tm, k), lambda j, i: (i, 0)),
                  pl.BlockSpec((1, k, tn), lambda j, i: (layer, 0, c0 + j))],
        out_specs=pl.BlockSpec((tm, tn), lambda j, i: (i, j)),
        out_shape=jax.ShapeDtypeStruct((m, n), F32),
        compiler_params=_cparams(("arbitrary", "arbitrary")),
        name="matmul",
    )(x, w)


def _ln_rows(z, g, b):
    mu = jnp.mean(z, axis=-1, keepdims=True)
    zc = z - mu
    var = jnp.mean(zc * zc, axis=-1, keepdims=True)
    return zc * lax.rsqrt(var + LN_EPS) * g + b


def _proj_ln_kernel(*refs, n_in, alpha):
    x_ref = refs[0]
    a_refs = refs[1:1 + n_in]
    w_refs = refs[1 + n_in:1 + 2 * n_in]
    g_ref, b_ref, o_ref = refs[1 + 2 * n_in:]
    acc = alpha * x_ref[...]
    for a_ref, w_ref in zip(a_refs, w_refs):
        acc = acc + jnp.dot(a_ref[...].astype(BF16), w_ref[...], preferred_element_type=F32)
    o_ref[...] = _ln_rows(acc, g_ref[...], b_ref[...])


def proj_residual_ln(x, acts, weights, g, b, alpha, tm=256):
    m, d = x.shape
    n_in = len(acts)
    in_specs = [pl.BlockSpec((tm, d), lambda i: (i, 0))]
    in_specs += [pl.BlockSpec((tm, a.shape[1]), lambda i: (i, 0)) for a in acts]
    in_specs += [pl.BlockSpec(w.shape, lambda i: (0, 0)) for w in weights]
    in_specs += [pl.BlockSpec((1, d), lambda i: (0, 0))] * 2
    return pl.pallas_call(
        functools.partial(_proj_ln_kernel, n_in=n_in, alpha=alpha),
        grid=(m // tm,),
        in_specs=in_specs,
        out_specs=pl.BlockSpec((tm, d), lambda i: (i, 0)),
        out_shape=jax.ShapeDtypeStruct((m, d), F32),
        compiler_params=_cparams(("arbitrary",)),
        name="proj_residual_ln",
    )(x, *acts, *weights, g.reshape(1, d), b.reshape(1, d))


def _router_kernel(x_ref, w_ref, b_ref, ti_ref, tg_ref, cnt_ref, *, n_exp, top_k):
    i = pl.program_id(0)

    @pl.when(i == 0)
    def _():
        cnt_ref[...] = jnp.zeros_like(cnt_ref)

    logits = jnp.dot(x_ref[...].astype(BF16), w_ref[...].astype(BF16),
                     preferred_element_type=F32) + b_ref[...]
    tm = logits.shape[0]
    lane = lax.broadcasted_iota(jnp.int32, logits.shape, 1)
    work = logits
    firsts, vals, sel = [], [], jnp.zeros(logits.shape, F32)
    for _ in range(top_k):
        mx = jnp.max(work, axis=-1, keepdims=True)
        first = jnp.min(jnp.where(work == mx, lane, n_exp), axis=-1, keepdims=True)
        pick = lane == first
        firsts.append(first)
        vals.append(mx)
        sel = sel + pick.astype(F32)
        work = jnp.where(pick, -jnp.inf, work)
    exps = [jnp.exp(v - vals[0]) for v in vals]
    den = exps[0]
    for e in exps[1:]:
        den = den + e
    row = lax.broadcasted_iota(jnp.int32, (tm, tm), 0)
    col = lax.broadcasted_iota(jnp.int32, (tm, tm), 1)
    tril = jnp.where(row > col, 1.0, 0.0).astype(BF16)
    before = jnp.dot(tril, sel.astype(BF16), preferred_element_type=F32) + cnt_ref[...]
    out_lane = lax.broadcasted_iota(jnp.int32, (tm, LANES), 1)
    ti = jnp.zeros((tm, LANES), jnp.int32)
    tg = jnp.zeros((tm, LANES), F32)
    for k in range(top_k):
        rank_k = jnp.sum(jnp.where(lane == firsts[k], before, 0.0), axis=-1, keepdims=True)
        ti = jnp.where(out_lane == k, firsts[k], ti)
        ti = jnp.where(out_lane == top_k + k, rank_k.astype(jnp.int32), ti)
        tg = jnp.where(out_lane == k, exps[k] / den, tg)
    ti_ref[...] = ti
    tg_ref[...] = tg
    cnt_ref[...] = cnt_ref[...] + jnp.sum(sel, axis=0, keepdims=True)


def moe_router(x, w_router, b_router):
    n, d = x.shape
    n_exp = w_router.shape[1]
    tm = min(ROUTER_TM, n)
    return pl.pallas_call(
        functools.partial(_router_kernel, n_exp=n_exp, top_k=TOP_K),
        grid=(n // tm,),
        in_specs=[pl.BlockSpec((tm, d), lambda i: (i, 0)),
                  pl.BlockSpec((d, n_exp), lambda i: (0, 0)),
                  pl.BlockSpec((1, n_exp), lambda i: (0, 0))],
        out_specs=[pl.BlockSpec((tm, LANES), lambda i: (i, 0)),
                   pl.BlockSpec((tm, LANES), lambda i: (i, 0)),
                   pl.BlockSpec((1, n_exp), lambda i: (0, 0))],
        out_shape=[jax.ShapeDtypeStruct((n, LANES), jnp.int32),
                   jax.ShapeDtypeStruct((n, LANES), F32),
                   jax.ShapeDtypeStruct((1, n_exp), F32)],
        compiler_params=_cparams(("arbitrary",)),
        name="moe_router",
    )(x, w_router, b_router.reshape(1, n_exp))


def _dispatch_kernel(dest_ref, pad_start_ref, pad_cnt_ref, x_ref, xs_hbm, zero_ref, sem, zsem,
                     *, top_k, n_exp, tb):
    i = pl.program_id(0)
    base = i * (tb * top_k)

    def row_copy(r, k):
        return pltpu.make_async_copy(x_ref.at[pl.ds(r, 1)],
                                     xs_hbm.at[pl.ds(dest_ref[base + r * top_k + k], 1)], sem)

    def for_rows(fn):
        def body(g, c):
            for u in range(DMA_UNROLL):
                for k in range(top_k):
                    fn(row_copy(g * DMA_UNROLL + u, k))
            return c
        lax.fori_loop(0, tb // DMA_UNROLL, body, 0)

    for_rows(lambda cp: cp.start())

    @pl.when(i == 0)
    def _():
        zero_ref[...] = jnp.zeros_like(zero_ref)

        def pad_copy(e, r):
            return pltpu.make_async_copy(zero_ref.at[pl.ds(0, 1)],
                                         xs_hbm.at[pl.ds(pad_start_ref[e] + r, 1)], zsem)

        def pad_body(e, carry):
            def issue(r, cc):
                pad_copy(e, r).start()
                return cc
            lax.fori_loop(0, pad_cnt_ref[e], issue, 0)

            def drain(r, cc):
                pad_copy(e, r).wait()
                return cc
            lax.fori_loop(0, pad_cnt_ref[e], drain, 0)
            return carry

        lax.fori_loop(0, n_exp, pad_body, 0)

    for_rows(lambda cp: cp.wait())


def moe_dispatch(x, dest, pad_start, pad_cnt, n_rows):
    n, d = x.shape
    n_exp = pad_start.shape[0]
    tb = min(DISPATCH_TB, n)
    return pl.pallas_call(
        functools.partial(_dispatch_kernel, top_k=TOP_K, n_exp=n_exp, tb=tb),
        grid_spec=pltpu.PrefetchScalarGridSpec(
            num_scalar_prefetch=3,
            grid=(n // tb,),
            in_specs=[pl.BlockSpec((tb, d), lambda i, de, ps, pc: (i, 0))],
            out_specs=pl.BlockSpec(memory_space=pl.ANY),
            scratch_shapes=[pltpu.VMEM((8, d), F32),
                            pltpu.SemaphoreType.DMA(()),
                            pltpu.SemaphoreType.DMA(())],
        ),
        out_shape=jax.ShapeDtypeStruct((n_rows, d), F32),
        compiler_params=_cparams(("arbitrary",)),
        name="moe_dispatch",
    )(dest, pad_start, pad_cnt, x)


def _expert_kernel(blk_e_ref, blk_rows_ref, blk_src_ref,
                   x_ref, wg_ref, wu_ref, bg_ref, bu_ref, wd_ref, bd_ref, o_ref,
                   xb_s, h_s, wg_s, wu_s, wd_s, *, n_f):
    s = pl.program_id(0)
    t = pl.program_id(1)
    rows = blk_rows_ref[s]
    n_sub = MOE_SUPER // MOE_SUB

    def for_sub_blocks(fn):
        for r in range(n_sub):
            @pl.when(r * MOE_SUB < rows)
            def _():
                fn(pl.ds(r * MOE_SUB, MOE_SUB))

    @pl.when((rows > 0) & (t == 0))
    def _():
        def cast_x(sl):
            xb_s[sl, :] = x_ref[sl, :].astype(BF16)
        for_sub_blocks(cast_x)

    @pl.when((rows > 0) & (t < n_f))
    def _():
        wg_s[...] = wg_ref[0, 0].astype(BF16)
        wu_s[...] = wu_ref[0, 0].astype(BF16)

        def up_proj(sl):
            xb = xb_s[sl, :]
            hg = jnp.dot(xb, wg_s[...], preferred_element_type=F32) + bg_ref[0, 0]
            hu = jnp.dot(xb, wu_s[...], preferred_element_type=F32) + bu_ref[0, 0]
            gl = jnp.minimum(hg, SWIGLU_LIMIT)
            up = jnp.clip(hu, -SWIGLU_LIMIT, SWIGLU_LIMIT)
            h_s[t, sl, :] = (gl * jax.nn.sigmoid(SWIGLU_ALPHA * gl) * (up + 1.0)).astype(BF16)
        for_sub_blocks(up_proj)

    @pl.when((rows > 0) & (t >= n_f))
    def _():
        wd_s[...] = wd_ref[0, 0].astype(BF16)

        def down_proj(sl):
            h = jnp.concatenate([h_s[j, sl, :] for j in range(n_f)], axis=1)
            o_ref[sl, :] = jnp.dot(h, wd_s[...], preferred_element_type=F32) + bd_ref[0, 0]
        for_sub_blocks(down_proj)

        for r in range(1, n_sub):
            @pl.when(r * MOE_SUB >= rows)
            def _():
                o_ref[pl.ds(r * MOE_SUB, MOE_SUB), :] = jnp.zeros((MOE_SUB, o_ref.shape[1]), F32)


def moe_experts(xs, blk_e, blk_rows, blk_src, w_gu, b_gu, w_dn, b_dn, layer):
    n_rows, d = xs.shape
    n_layers, n_exp, _, two_f = w_gu.shape
    d_ff = two_f // 2
    n_super = n_rows // MOE_SUPER
    n_f = d_ff // MOE_TF
    n_d = d // MOE_TN

    def j1(s, t, rows):
        return jnp.where(rows[s] > 0, jnp.minimum(t, n_f - 1), n_f - 1)

    def j2(s, t, rows):
        return jnp.where(rows[s] > 0, jnp.maximum(t - n_f, 0), n_d - 1)

    return pl.pallas_call(
        functools.partial(_expert_kernel, n_f=n_f),
        grid_spec=pltpu.PrefetchScalarGridSpec(
            num_scalar_prefetch=3,
            grid=(n_super, n_f + n_d),
            in_specs=[
                pl.BlockSpec((MOE_SUPER, d), lambda s, t, be, br, bs: (bs[s], 0)),
                pl.BlockSpec((1, 1, d, MOE_TF),
                             lambda s, t, be, br, bs: (layer, be[s], 0, j1(s, t, br))),
                pl.BlockSpec((1, 1, d, MOE_TF),
                             lambda s, t, be, br, bs: (layer, be[s], 0, n_f + j1(s, t, br))),
                pl.BlockSpec((1, 1, 1, MOE_TF),
                             lambda s, t, be, br, bs: (layer, be[s], 0, j1(s, t, br))),
                pl.BlockSpec((1, 1, 1, MOE_TF),
                             lambda s, t, be, br, bs: (layer, be[s], 0, n_f + j1(s, t, br))),
                pl.BlockSpec((1, 1, d_ff, MOE_TN),
                             lambda s, t, be, br, bs: (layer, be[s], 0, j2(s, t, br))),
                pl.BlockSpec((1, 1, 1, MOE_TN),
                             lambda s, t, be, br, bs: (layer, be[s], 0, j2(s, t, br))),
            ],
            out_specs=pl.BlockSpec((MOE_SUPER, MOE_TN), lambda s, t, be, br, bs: (bs[s], j2(s, t, br))),
            scratch_shapes=[pltpu.VMEM((MOE_SUPER, d), BF16),
                            pltpu.VMEM((n_f, MOE_SUPER, MOE_TF), BF16),
                            pltpu.VMEM((d, MOE_TF), BF16),
                            pltpu.VMEM((d, MOE_TF), BF16),
                            pltpu.VMEM((d_ff, MOE_TN), BF16)],
        ),
        out_shape=jax.ShapeDtypeStruct((n_rows, d), F32),
        compiler_params=_cparams(("arbitrary", "arbitrary")),
        name="moe_experts",
    )(blk_e, blk_rows, blk_src, xs, w_gu, w_gu, b_gu.reshape(n_layers, n_exp, 1, two_f),
      b_gu.reshape(n_layers, n_exp, 1, two_f), w_dn, b_dn.reshape(n_layers, n_exp, 1, d))


def _combine_kernel(dest_ref, x_ref, tg_ref, g_ref, b_ref, yb_hbm, o_ref, buf, sem,
                    *, top_k, alpha, tb):
    i = pl.program_id(0)
    n_blk = pl.num_programs(0)

    def for_rows(blk, slot, fn):
        base = blk * (tb * top_k)

        def body(g, c):
            for u in range(DMA_UNROLL):
                r = g * DMA_UNROLL + u
                for k in range(top_k):
                    fn(pltpu.make_async_copy(
                        yb_hbm.at[pl.ds(dest_ref[base + r * top_k + k], 1)],
                        buf.at[slot, k, pl.ds(r, 1)], sem.at[slot]))
            return c
        lax.fori_loop(0, tb // DMA_UNROLL, body, 0)

    @pl.when(i == 0)
    def _():
        for_rows(0, 0, lambda cp: cp.start())

    @pl.when(i + 1 < n_blk)
    def _():
        for_rows(i + 1, (i + 1) % 2, lambda cp: cp.start())

    slot = i % 2
    for_rows(i, slot, lambda cp: cp.wait())

    tg = tg_ref[...]
    acc = alpha * x_ref[...]
    for k in range(top_k):
        acc = acc + tg[:, k:k + 1] * buf[slot, k]
    o_ref[...] = _ln_rows(acc, g_ref[...], b_ref[...])


def moe_combine_ln(x, yb, dest, tg, g, b, alpha):
    n, d = x.shape
    tb = COMBINE_TB
    return pl.pallas_call(
        functools.partial(_combine_kernel, top_k=TOP_K, alpha=alpha, tb=tb),
        grid_spec=pltpu.PrefetchScalarGridSpec(
            num_scalar_prefetch=1,
            grid=(n // tb,),
            in_specs=[pl.BlockSpec((tb, d), lambda i, dr: (i, 0)),
                      pl.BlockSpec((tb, LANES), lambda i, dr: (i, 0)),
                      pl.BlockSpec((1, d), lambda i, dr: (0, 0)),
                      pl.BlockSpec((1, d), lambda i, dr: (0, 0)),
                      pl.BlockSpec(memory_space=pl.ANY)],
            out_specs=pl.BlockSpec((tb, d), lambda i, dr: (i, 0)),
            scratch_shapes=[pltpu.VMEM((2, TOP_K, tb, d), F32),
                            pltpu.SemaphoreType.DMA((2,))],
        ),
        out_shape=jax.ShapeDtypeStruct((n, d), F32),
        compiler_params=_cparams(("arbitrary",)),
        name="moe_combine_ln",
    )(dest, x, tg, g.reshape(1, d), b.reshape(1, d), yb)


def moe_layer(x, w_router, b_router, w_gu, b_gu, w_dn, b_dn, layer, ln_g, ln_b, alpha):
    n, d = x.shape
    n_exp = w_router.shape[1]
    m = n * TOP_K
    n_super = -(-m // MOE_SUPER) + n_exp
    n_rows = n_super * MOE_SUPER

    ti, tg, cnt = moe_router(x, w_router, b_router)
    counts = cnt[0].astype(jnp.int32)
    nsup = (counts + MOE_SUPER - 1) // MOE_SUPER
    sup_end = jnp.cumsum(nsup)
    sup_start = sup_end - nsup
    row_start = sup_start * MOE_SUPER
    top_i = ti[:, :TOP_K]
    rank = ti[:, TOP_K:2 * TOP_K]
    dest = (row_start[top_i] + rank).reshape(m)
    s_idx = jnp.arange(n_super, dtype=jnp.int32)
    n_used = sup_end[-1]
    used = s_idx < n_used
    src = jnp.where(used, s_idx, n_used - 1)
    blk_e = jnp.minimum(jnp.searchsorted(sup_end, src, side="right"), n_exp - 1).astype(jnp.int32)
    blk_rows = jnp.clip(counts[blk_e] - (src - sup_start[blk_e]) * MOE_SUPER, 0, MOE_SUPER)
    blk_rows = jnp.where(used, blk_rows, 0).astype(jnp.int32)
    pad_start = row_start + counts
    pad_cnt = (-counts) % MOE_SUB

    xs = moe_dispatch(x, dest, pad_start, pad_cnt, n_rows)
    yb = moe_experts(xs, blk_e, blk_rows, src.astype(jnp.int32), w_gu, b_gu, w_dn, b_dn, layer)
    return moe_combine_ln(x, yb, dest, tg, ln_g, ln_b, alpha)


def _head_sum_matrix(n_b, scale):
    row = lax.broadcasted_iota(jnp.int32, (LANES, LANES), 0)
    col = lax.broadcasted_iota(jnp.int32, (LANES, LANES), 1)
    return jnp.where(row // n_b == col // n_b, scale, 0.0).astype(BF16)


def _split_dot(dot_fn, x, terms=3):
    acc = None
    for _ in range(terms):
        piece = x.astype(BF16)
        part = dot_fn(piece)
        acc = part if acc is None else acc + part
        x = x - piece.astype(F32)
    return acc


def _per_head(x, mat):
    outs = [_split_dot(lambda a: jnp.dot(a, mat, preferred_element_type=F32),
                       x[:, s * LANES:(s + 1) * LANES]) for s in range(x.shape[1] // LANES)]
    return jnp.concatenate(outs, axis=1)


def _rwkv_prep_kernel(p_ref, prev_ref, mu_ref, w0_ref, w2_ref, a0_ref, a2_ref, g2_ref, kk_ref,
                      ka_ref, r_ref, k_ref, v_ref, lw_ref, al_ref, be_ref, g_ref,
                      *, d_b, wl, al, n_b):
    p = p_ref[...]
    xm = p + (prev_ref[...] - p) * mu_ref[...]
    r = xm[:, :d_b]
    k = xm[:, d_b:2 * d_b]
    v = xm[:, 2 * d_b:3 * d_b]
    wd = xm[:, 3 * d_b:3 * d_b + wl]
    ad = xm[:, 3 * d_b + wl:3 * d_b + wl + al]
    gd = xm[:, 3 * d_b + wl + al:]
    dotb = lambda a, w_ref: jnp.dot(a.astype(BF16), w_ref[...].astype(BF16),
                                    preferred_element_type=F32)
    wlog = -jax.nn.softplus(-(w0_ref[...] + dotb(jnp.tanh(wd), w2_ref))) - 0.5
    a = jax.nn.sigmoid(a0_ref[...] + dotb(ad, a2_ref))
    kk = k * kk_ref[...]
    norm = jnp.sqrt(_per_head(kk * kk, _head_sum_matrix(n_b, 1.0)))
    kk = kk / jnp.maximum(norm, 1e-12)
    r_ref[...] = r
    k_ref[...] = k * (1.0 + (a - 1.0) * ka_ref[...])
    v_ref[...] = v
    lw_ref[...] = -jnp.exp(wlog)
    al_ref[...] = -kk
    be_ref[...] = kk * a
    g_ref[...] = dotb(jax.nn.sigmoid(gd), g2_ref)


def rwkv_prep(p, prev, mu, w0, w2, a0, a2, g2, k_k, k_a, n_b, tm=256):
    n, pb = p.shape
    d_b = w0.shape[0]
    wl, al = w2.shape[0], a2.shape[0]
    row = lambda c: pl.BlockSpec((tm, c), lambda i: (i, 0))
    full = lambda a: pl.BlockSpec(a.shape, lambda i: (0, 0))
    vec = lambda a: a.reshape(1, -1)
    consts = [vec(mu), vec(w0), w2, vec(a0), a2, g2, vec(k_k), vec(k_a)]
    return pl.pallas_call(
        functools.partial(_rwkv_prep_kernel, d_b=d_b, wl=wl, al=al, n_b=n_b),
        grid=(n // tm,),
        in_specs=[row(pb), row(pb)] + [full(c) for c in consts],
        out_specs=[row(d_b)] * 7,
        out_shape=[jax.ShapeDtypeStruct((n, d_b), F32)] * 7,
        compiler_params=_cparams(("arbitrary",)),
        name="rwkv_prep",
    )(p, prev, *consts)


def _rwkv_chunk_kernel(r_ref, k_ref, v_ref, lw_ref, al_ref, be_ref, g_ref, s0_ref, gng_ref,
                       gnb_ref, rk_ref, o_ref, s_out_ref, st_ref, *, n_b, chunk, n_pairs):
    c = pl.program_id(1)
    c2 = 2 * chunk

    @pl.when(c == 0)
    def _():
        st_ref[...] = s0_ref[0]

    left = lambda mat: (lambda a: jnp.dot(mat, a, preferred_element_type=F32))
    right = lambda mat: (lambda a: jnp.dot(a, mat, preferred_element_type=F32))
    dot = lambda a, b: jnp.dot(a.astype(BF16), b.astype(BF16), preferred_element_type=F32)
    dot_nt = lambda a, b: lax.dot_general(a.astype(BF16), b.astype(BF16), (((1,), (1,)), ((), ())),
                                          preferred_element_type=F32)
    dot_tn = lambda a, b: lax.dot_general(a.astype(BF16), b.astype(BF16), (((0,), (0,)), ((), ())),
                                          preferred_element_type=F32)
    lane = lax.broadcasted_iota(jnp.int32, (chunk, LANES), 1)
    head_a = lane < n_b
    stack = lambda x: jnp.concatenate([jnp.where(head_a, x, 0.0), jnp.where(head_a, 0.0, x)], axis=0)
    twice = lambda x: jnp.concatenate([x, x], axis=0)
    row2 = lax.broadcasted_iota(jnp.int32, (c2, c2), 0)
    col2 = lax.broadcasted_iota(jnp.int32, (c2, c2), 1)
    same = (row2 // chunk) == (col2 // chunk)
    strict = same & (row2 > col2)
    incl = same & (row2 >= col2)
    rowc = lax.broadcasted_iota(jnp.int32, (chunk, chunk), 0)
    colc = lax.broadcasted_iota(jnp.int32, (chunk, chunk), 1)
    cum = jnp.where(rowc >= colc, 1.0, 0.0).astype(BF16)
    rl = lax.broadcasted_iota(jnp.int32, (LANES, LANES), 0)
    cl_ = lax.broadcasted_iota(jnp.int32, (LANES, LANES), 1)
    blockdiag = (rl // n_b) == (cl_ // n_b)
    eye = rl == cl_
    mean_mat = _head_sum_matrix(n_b, 1.0 / n_b)
    sum_mat = _head_sum_matrix(n_b, 1.0)

    for p in range(n_pairs):
        sl = slice(p * LANES, (p + 1) * LANES)
        r, k, v = r_ref[:, sl], k_ref[:, sl], v_ref[:, sl]
        lw, al, be = lw_ref[:, sl], al_ref[:, sl], be_ref[:, sl]
        cl = _split_dot(left(cum), lw)
        e_neg = jnp.exp(-cl)
        at = al * jnp.exp(cl - lw)
        rt = r * jnp.exp(cl)
        bt = be * e_neg
        kt = k * e_neg
        gram = dot_nt(jnp.concatenate([stack(at), stack(rt)], axis=0),
                      jnp.concatenate([twice(bt), twice(kt)], axis=0))
        nm = jnp.where(strict, gram[:c2, :c2], 0.0)
        aak = jnp.where(strict, gram[:c2, c2:], 0.0)
        arb = jnp.where(incl, gram[c2:, :c2], 0.0)
        ark = jnp.where(incl, gram[c2:, c2:], 0.0)
        m = st_ref[p]
        vs = stack(v)
        from_state = dot(jnp.concatenate([at, rt], axis=0), m)
        u = stack(from_state[:chunk]) + dot(aak, vs)
        pw = nm
        n = 1
        while n < chunk:
            u = u + dot(pw, u)
            n *= 2
            if n < chunk:
                pw = dot(pw, pw)
        ys = stack(from_state[chunk:]) + dot(jnp.concatenate([arb, ark], axis=1),
                                             jnp.concatenate([u, vs], axis=0))
        y = ys[:chunk] + ys[chunk:]
        up = u[:chunk] + u[chunk:]
        to_end = jnp.exp(cl[chunk - 1:chunk, :] - cl)
        pc_col = jnp.transpose(jnp.broadcast_to(jnp.exp(cl[chunk - 1:chunk, :]), (8, LANES)))[:, 0:1]
        upd = dot_tn(jnp.concatenate([be * to_end, k * to_end], axis=0),
                     jnp.concatenate([up, v], axis=0))
        st_ref[p] = pc_col * m + jnp.where(blockdiag, upd, 0.0)

        stats = _split_dot(right(sum_mat), jnp.concatenate([y, y * y, r * k * rk_ref[:, sl]], axis=0))
        mean = stats[:chunk] * (1.0 / n_b)
        var = stats[chunk:c2] * (1.0 / n_b) - mean * mean
        yn = (y - mean) * lax.rsqrt(var + RWKV_GN_EPS) * gng_ref[:, sl] + gnb_ref[:, sl]
        o_ref[:, sl] = (yn + stats[c2:] * v) * g_ref[:, sl]

    @pl.when(c == pl.num_programs(1) - 1)
    def _():
        s_out_ref[0] = st_ref[...]


def rwkv_chunked(r, k, v, lw, al, be, g, s0_bd, gn_g, gn_b, r_k, bsz, t, n_b, chunk):
    n, d_b = r.shape
    n_pairs = d_b // LANES
    n_c = t // chunk
    row = pl.BlockSpec((chunk, d_b), lambda b, c: (b * n_c + c, 0))
    vec = pl.BlockSpec((1, d_b), lambda b, c: (0, 0))
    st = pl.BlockSpec((1, n_pairs, LANES, LANES), lambda b, c: (b, 0, 0, 0))
    return pl.pallas_call(
        functools.partial(_rwkv_chunk_kernel, n_b=n_b, chunk=chunk, n_pairs=n_pairs),
        grid=(bsz, n_c),
        in_specs=[row] * 7 + [st, vec, vec, vec],
        out_specs=[row, st],
        out_shape=[jax.ShapeDtypeStruct((n, d_b), F32),
                   jax.ShapeDtypeStruct(s0_bd.shape, F32)],
        scratch_shapes=[pltpu.VMEM((n_pairs, LANES, LANES), F32)],
        compiler_params=_cparams(("arbitrary", "arbitrary")),
        name="rwkv_chunked",
    )(r, k, v, lw, al, be, g, s0_bd, gn_g.reshape(1, d_b), gn_b.reshape(1, d_b),
      r_k.reshape(1, d_b))


def _state_to_blockdiag(s):
    bsz, n_h, n_v, n_k = s.shape
    st = jnp.swapaxes(s, 2, 3).reshape(bsz, n_h // 2, 2, n_k, n_v)
    z = jnp.zeros_like(st[:, :, 0])
    top = jnp.concatenate([st[:, :, 0], z], axis=-1)
    bot = jnp.concatenate([z, st[:, :, 1]], axis=-1)
    return jnp.concatenate([top, bot], axis=-2)


def _blockdiag_to_state(m, n_b):
    bsz, n_pairs = m.shape[:2]
    a = m[:, :, :n_b, :n_b]
    b = m[:, :, n_b:, n_b:]
    st = jnp.stack([a, b], axis=2).reshape(bsz, n_pairs * 2, n_b, n_b)
    return jnp.swapaxes(st, 2, 3)


def rwkv7_group(p, shift_prev, s0, mu, w0, w2, a0, a2, g2, k_k, k_a, r_k, gn_g, gn_b, chunk):
    bsz, t, pb = p.shape
    n_b = s0.shape[2]
    prev = jnp.concatenate([shift_prev[:, None], p[:, :-1]], axis=1).reshape(bsz * t, pb)
    tm = 256 if (bsz * t) % 256 == 0 else bsz * t
    r, k, v, lw, al, be, g = rwkv_prep(p.reshape(bsz * t, pb), prev, mu, w0, w2, a0, a2, g2,
                                       k_k, k_a, n_b, tm=tm)
    o, m = rwkv_chunked(r, k, v, lw, al, be, g, _state_to_blockdiag(s0), gn_g, gn_b, r_k,
                        bsz, t, n_b, chunk)
    return o, _blockdiag_to_state(m, n_b), p[:, -1]


def _hgrn_kernel(q_ref, f_ref, i_ref, gate_ref, lb_ref, ng_ref, s0_ref, o_ref, s_out_ref, st_ref,
                 *, n_h, sub, tb):
    c = pl.program_id(1)

    @pl.when(c == 0)
    def _():
        st_ref[...] = s0_ref[0]

    row = lax.broadcasted_iota(jnp.int32, (sub, sub), 0)
    col = lax.broadcasted_iota(jnp.int32, (sub, sub), 1)
    cum = jnp.where(row >= col, 1.0, 0.0).astype(BF16)
    t_idx = lax.broadcasted_iota(jnp.int32, (sub, LANES), 0)
    bf = lambda x: x.astype(BF16)

    def body(j, carry):
        rows = pl.ds(pl.multiple_of(j * sub, sub), sub)
        for h in range(n_h):
            sl = slice(h * LANES, (h + 1) * LANES)
            lb = lb_ref[:, sl]
            s_mat = st_ref[h]
            q = jax.nn.silu(q_ref[rows, sl])
            z = f_ref[rows, sl]
            logf = jnp.log(lb + (1.0 - lb) * jax.nn.sigmoid(z))
            kk = (1.0 - lb) * jax.nn.sigmoid(-z)
            v = i_ref[rows, sl]
            b = _split_dot(lambda a: jnp.dot(cum, a, preferred_element_type=F32), logf)
            o = jnp.dot(bf(q * jnp.exp(b)), bf(s_mat), preferred_element_type=F32)
            vb = bf(v).astype(F32)
            for s in range(sub):
                dec = jnp.exp(jnp.where(t_idx >= s, b - b[s:s + 1, :], -jnp.inf))
                att = jnp.sum(q * kk[s:s + 1, :] * dec, axis=-1, keepdims=True)
                o = o + bf(att).astype(F32) * vb[s:s + 1, :]
            bl = b[sub - 1:sub, :]
            kd = bf(kk * jnp.exp(bl - b))
            upd = lax.dot_general(kd, bf(v), (((0,), (0,)), ((), ())), preferred_element_type=F32)
            decay = jnp.exp(bl)
            st_ref[h] = jnp.transpose(jnp.broadcast_to(decay, (8, LANES)))[:, 0:1] * s_mat + upd
            o = o * lax.rsqrt(jnp.mean(o * o, axis=-1, keepdims=True) + LN_EPS) * ng_ref[...]
            o_ref[rows, sl] = o * jax.nn.silu(gate_ref[rows, sl])
        return carry

    lax.fori_loop(0, tb // sub, body, 0)

    @pl.when(c == pl.num_programs(1) - 1)
    def _():
        s_out_ref[0] = st_ref[...]


def hgrn2(proj, s0, lb, norm_g, bsz, t):
    n = proj.shape[0]
    n_h, dk, dv = s0.shape[1:]
    d_a = n_h * dk
    sub = min(CHUNK_A, t)
    tb = min(256, t)
    n_c = t // tb
    col = lambda which: pl.BlockSpec((tb, d_a), lambda b, c: (b * n_c + c, which))
    st = pl.BlockSpec((1, n_h, dk, dv), lambda b, c: (b, 0, 0, 0))
    return pl.pallas_call(
        functools.partial(_hgrn_kernel, n_h=n_h, sub=sub, tb=tb),
        grid=(bsz, n_c),
        in_specs=[col(0), col(1), col(2), col(3),
                  pl.BlockSpec((1, d_a), lambda b, c: (0, 0)),
                  pl.BlockSpec((1, dv), lambda b, c: (0, 0)), st],
        out_specs=[pl.BlockSpec((tb, d_a), lambda b, c: (b * n_c + c, 0)), st],
        out_shape=[jax.ShapeDtypeStruct((n, d_a), F32), jax.ShapeDtypeStruct(s0.shape, F32)],
        scratch_shapes=[pltpu.VMEM((n_h, dk, dv), F32)],
        compiler_params=_cparams(("arbitrary", "arbitrary")),
        name="hgrn2",
    )(proj, proj, proj, proj, lb.reshape(1, d_a), norm_g.reshape(1, dv), s0)


def _pool_ln_kernel(x_ref, halo_ref, w_ref, sc_ref, g_ref, b_ref, o_ref, xx_ref,
                    *, windows, start_pos, alpha, tb, halo):
    c = pl.program_id(1)
    d = x_ref.shape[1]
    pc = d // len(windows)
    xx_ref[0:halo, :] = halo_ref[0, 0]
    xx_ref[halo:halo + tb, :] = x_ref[...]
    x = x_ref[...]
    pos = start_pos + c * tb + lax.broadcasted_iota(jnp.int32, (tb, 1), 0)
    ys = []
    for gi, w in enumerate(windows):
        sl = slice(gi * pc, (gi + 1) * pc)
        acc = x[:, sl]
        for back in range(1, w):
            acc = acc + xx_ref[halo - back:halo - back + tb, sl]
        cnt = jnp.minimum(pos + 1, w).astype(F32)
        u = acc / cnt - x[:, sl]
        ys.append(jnp.dot(u.astype(BF16), w_ref[gi], preferred_element_type=F32))
    y = jnp.concatenate(ys, axis=1) * sc_ref[...]
    o_ref[...] = _ln_rows(alpha * x + y, g_ref[...], b_ref[...])


def pool_residual_ln(x, hist, start_pos, w_pool, scale, g, b, alpha, bsz, t):
    n, d = x.shape
    halo = 16
    tb = min(256, t)
    n_c = t // tb
    x4 = x.reshape(bsz, n_c, tb, d)
    first = jnp.concatenate([jnp.zeros((bsz, halo - hist.shape[1], d), F32), hist], axis=1)
    halos = first[:, None]
    if n_c > 1:
        halos = jnp.concatenate([halos, x4[:, :-1, tb - halo:]], axis=1)
    vec = lambda a: pl.BlockSpec((1, d), lambda bi, c: (0, 0))
    return pl.pallas_call(
        functools.partial(_pool_ln_kernel, windows=POOL_WINDOWS, start_pos=start_pos,
                          alpha=alpha, tb=tb, halo=halo),
        grid=(bsz, n_c),
        in_specs=[pl.BlockSpec((tb, d), lambda bi, c: (bi * n_c + c, 0)),
                  pl.BlockSpec((1, 1, halo, d), lambda bi, c: (bi, c, 0, 0)),
                  pl.BlockSpec(w_pool.shape, lambda bi, c: (0, 0, 0)),
                  vec(scale), vec(g), vec(b)],
        out_specs=pl.BlockSpec((tb, d), lambda bi, c: (bi * n_c + c, 0)),
        out_shape=jax.ShapeDtypeStruct((n, d), F32),
        scratch_shapes=[pltpu.VMEM((halo + tb, d), F32)],
        compiler_params=_cparams(("arbitrary", "arbitrary")),
        name="pool_residual_ln",
    )(x, halos, w_pool.astype(BF16), scale.reshape(1, d), g.reshape(1, d), b.reshape(1, d))


def _hgrn2_group(q, f, i, gate, s0, lb, norm_g):
    bsz, t, d_a = q.shape
    n_h, dk, dv = s0.shape[1:]
    q = jax.nn.silu(q)
    logf = jnp.log(lb + (1.0 - lb) * jax.nn.sigmoid(f))
    k = (1.0 - lb) * jax.nn.sigmoid(-f)
    tp = -(-t // CHUNK_A) * CHUNK_A
    pad = [(0, 0), (0, tp - t), (0, 0)]
    q, k, v, logf = [jnp.pad(a, pad) for a in (q, k, i, logf)]
    nc = tp // CHUNK_A
    blocks = lambda a, dd: a.reshape(bsz, nc, CHUNK_A, n_h, dd).transpose(1, 0, 3, 2, 4)
    mask = jnp.tril(jnp.ones((CHUNK_A, CHUNK_A), bool))

    def step(S, inp):
        qc, kc, vc, gc = inp
        b = jnp.cumsum(gc, axis=2)
        diff = b[:, :, :, None, :] - b[:, :, None, :, :]
        dec = jnp.exp(jnp.where(mask[:, :, None], diff, -jnp.inf))
        att = jnp.sum(qc[:, :, :, None, :] * kc[:, :, None, :, :] * dec, axis=-1)
        o = (jnp.einsum('bhts,bhsv->bhtv', att, vc)
             + jnp.einsum('bhtd,bhdv->bhtv', qc * jnp.exp(b), S))
        bl = b[:, :, -1:, :]
        S = (jnp.exp(bl[:, :, 0, :])[..., None] * S
             + jnp.einsum('bhsd,bhsv->bhdv', kc * jnp.exp(bl - b), vc))
        return S, o

    s_t, o = lax.scan(step, s0, (blocks(q, dk), blocks(k, dk), blocks(v, dv), blocks(logf, dk)))
    o = o.transpose(1, 0, 3, 2, 4).reshape(bsz, tp, n_h, dv)[:, :t]
    o = o * lax.rsqrt(jnp.mean(o * o, -1, keepdims=True) + LN_EPS) * norm_g
    return o.reshape(bsz, t, d_a) * jax.nn.silu(gate), s_t


def _rwkv7_group(p, shift_prev, s0, mu, w0, w2, a0, a2, g2, k_k, k_a, r_k, gn_g, gn_b, chunk):
    bsz, t, _ = p.shape
    d_b = w0.shape[0]
    n_h, n_b = s0.shape[1], s0.shape[2]
    wl, al_ = w2.shape[0], a2.shape[0]
    prev = jnp.concatenate([shift_prev[:, None], p[:, :-1]], axis=1)
    xm = p + (prev - p) * mu
    r, k, v, wd, ad, gd = jnp.split(
        xm, [d_b, 2 * d_b, 3 * d_b, 3 * d_b + wl, 3 * d_b + wl + al_], axis=-1)
    wlog = -jax.nn.softplus(-(w0 + jnp.tanh(wd) @ w2)) - 0.5
    logw = -jnp.exp(wlog)
    a = jax.nn.sigmoid(a0 + ad @ a2)
    g = jax.nn.sigmoid(gd) @ g2
    hs = lambda x: x.reshape(bsz, t, n_h, n_b)
    r, k, v, logw, a = map(hs, (r, k, v, logw, a))
    kk = k * k_k.reshape(n_h, n_b)
    kk = kk / jnp.maximum(jnp.sqrt(jnp.sum(kk * kk, -1, keepdims=True)), 1e-12)
    k2 = k * (1.0 + (a - 1.0) * k_a.reshape(n_h, n_b))
    al = -kk
    be = kk * a
    c = chunk
    nc = t // c
    ch = lambda x: x.reshape(bsz, nc, c, n_h, n_b).transpose(1, 0, 3, 2, 4)
    strict = jnp.tril(jnp.ones((c, c), bool), -1)
    incl = jnp.tril(jnp.ones((c, c), bool))
    ein = functools.partial(jnp.einsum, precision=HIGHEST)

    def step(S, inp):
        rc, kc, vc, lc, ac, bc = inp
        cl = jnp.cumsum(lc, axis=2)
        at = ac * jnp.exp(cl - lc)
        rt = rc * jnp.exp(cl)
        bt = bc * jnp.exp(-cl)
        kt = kc * jnp.exp(-cl)
        nm = jnp.where(strict, ein('bhck,bhik->bhci', at, bt), 0.0)
        aak = jnp.where(strict, ein('bhck,bhik->bhci', at, kt), 0.0)
        arb = jnp.where(incl, ein('bhck,bhik->bhci', rt, bt), 0.0)
        ark = jnp.where(incl, ein('bhck,bhik->bhci', rt, kt), 0.0)
        u = ein('bhck,bhvk->bhcv', at, S) + ein('bhci,bhiv->bhcv', aak, vc)
        pw = nm
        n = 1
        while n < c:
            u = u + ein('bhci,bhiv->bhcv', pw, u)
            n *= 2
            if n < c:
                pw = ein('bhci,bhij->bhcj', pw, pw)
        y = (ein('bhck,bhvk->bhcv', rt, S) + ein('bhci,bhiv->bhcv', arb, u)
             + ein('bhci,bhiv->bhcv', ark, vc))
        sn = jnp.exp(cl[:, :, -1])[:, :, None, :] * (
            S + ein('bhiv,bhik->bhvk', u, bt) + ein('bhiv,bhik->bhvk', vc, kt))
        return sn, y

    s_t, y = lax.scan(step, s0, tuple(map(ch, (r, k2, v, logw, al, be))))
    y = y.transpose(1, 0, 3, 2, 4).reshape(bsz, t, n_h, n_b)
    mean = jnp.mean(y, -1, keepdims=True)
    var = jnp.mean(jnp.square(y - mean), -1, keepdims=True)
    y = (y - mean) * lax.rsqrt(var + RWKV_GN_EPS) * gn_g.reshape(n_h, n_b) + gn_b.reshape(n_h, n_b)
    y = y + jnp.sum(r * k2 * r_k, -1, keepdims=True) * v
    return y.reshape(bsz, t, d_b) * g, s_t, p[:, -1]


def _pool_group(x, hist, start_pos, w_pool, scale):
    bsz, t, d = x.shape
    n_g = len(POOL_WINDOWS)
    pc = d // n_g
    hlen = hist.shape[1]
    xx = jnp.concatenate([hist, x], axis=1)
    pos = jnp.arange(t) + start_pos
    outs = []
    for gi, w in enumerate(POOL_WINDOWS):
        sl = slice(gi * pc, (gi + 1) * pc)
        acc = x[..., sl]
        for back in range(1, w):
            acc = acc + xx[:, hlen - back:hlen - back + t, sl]
        cnt = jnp.minimum(pos + 1, w).astype(F32)[None, :, None]
        outs.append(acc / cnt - x[..., sl])
    u = jnp.stack(outs, axis=2)
    y = jnp.einsum('btgc,gcd->btgd', u, w_pool).reshape(bsz, t, d) * scale
    return y, xx[:, -hlen:]


def kernel(x_prompt, x_sample, state_hgrn, state_rwkv, state_rwkv_shift, state_pool, mix_w_in, hgrn_lb, hgrn_norm_g, rwkv_mu, rwkv_w0, rwkv_w2, rwkv_a0, rwkv_a2, rwkv_g2, rwkv_k_k, rwkv_k_a, rwkv_r_k, rwkv_gn_g, rwkv_gn_b, mix_w_out, pool_w, pool_scale, ln1_g, ln1_b, ln2_g, ln2_b, moe_w_router, moe_b_router, moe_w_gu, moe_b_gu, moe_w_dn, moe_b_dn):
    bp, tp, d = x_prompt.shape
    bs, ts, _ = x_sample.shape
    depth = ln1_g.shape[0]
    n_h_a, dk_a, dv_a = state_hgrn.shape[2:]
    n_h_b, n_b = state_rwkv.shape[2:4]
    d_a = n_h_a * dk_a
    pb = state_rwkv_shift.shape[2]
    hist = state_pool.shape[2]
    n_p = bp * tp
    alpha = (2 * depth) ** 0.25

    lb_p = jax.nn.softmax(hgrn_lb, axis=0)
    lb_all = jnp.cumsum(lb_p, axis=0) - lb_p[0]
    x = jnp.concatenate([x_prompt.reshape(n_p, d), x_sample.reshape(bs * ts, d)], axis=0)
    hg_p, hg_s, rw_p, rw_s, sh_p, sh_s, pl_p, pl_s = [], [], [], [], [], [], [], []
    for l in range(depth):
        if l % 2 == 0:
            e = l // 2
            p_a = matmul(x, mix_w_in, e, 0, 4 * d_a)
            p_b = matmul(x, mix_w_in, e, 4 * d_a, pb)
            rw = (rwkv_mu[e], rwkv_w0[e], rwkv_w2[e], rwkv_a0[e], rwkv_a2[e], rwkv_g2[e],
                  rwkv_k_k[e], rwkv_k_a[e], rwkv_r_k[e], rwkv_gn_g[e], rwkv_gn_b[e])
            outs = []
            for grp, (lo, hi, bsz, t) in enumerate(((0, n_p, bp, tp), (n_p, n_p + bs * ts, bs, ts))):
                if grp == 0:
                    s_h = jnp.zeros((bsz, n_h_a, dk_a, dv_a), F32)
                    s_r = jnp.zeros((bsz, n_h_b, n_b, n_b), F32)
                    s_s = jnp.zeros((bsz, pb), F32)
                else:
                    s_h, s_r, s_s = state_hgrn[e], state_rwkv[e], state_rwkv_shift[e]
                o_a, n_h = hgrn2(p_a[lo:hi], s_h, lb_all[e], hgrn_norm_g[e], bsz, t)
                o_b, n_r, n_s = rwkv7_group(p_b[lo:hi].reshape(bsz, t, pb), s_s, s_r, *rw,
                                            chunk=min(t, RWKV_CHUNK))
                outs.append((o_a, o_b))
                (hg_p, hg_s)[grp].append(n_h)
                (rw_p, rw_s)[grp].append(n_r)
                (sh_p, sh_s)[grp].append(n_s)
            o_a = jnp.concatenate([outs[0][0], outs[1][0]], axis=0)
            o_b = jnp.concatenate([outs[0][1], outs[1][1]], axis=0)
            w_out = mix_w_out[e].astype(BF16)
            x = proj_residual_ln(x, [o_a, o_b], [w_out[:d_a], w_out[d_a:]], ln1_g[l], ln1_b[l], alpha)
        else:
            j = l // 2
            x_p, x_s = x[:n_p], x[n_p:]
            hist_p = jnp.zeros((bp, hist, d), F32)
            pl_p.append(jnp.concatenate([hist_p, x_p.reshape(bp, tp, d)], axis=1)[:, -hist:])
            pl_s.append(jnp.concatenate([state_pool[j], x_s.reshape(bs, ts, d)], axis=1)[:, -hist:])
            pool = (pool_w[j], pool_scale[j], ln1_g[l], ln1_b[l], alpha)
            x = jnp.concatenate([pool_residual_ln(x_p, hist_p, 0, *pool, bp, tp),
                                 pool_residual_ln(x_s, state_pool[j], PAST_LEN, *pool, bs, ts)], axis=0)
        x = moe_layer(x, moe_w_router[l], moe_b_router[l], moe_w_gu, moe_b_gu, moe_w_dn, moe_b_dn, l,
                      ln2_g[l], ln2_b[l], alpha)
    return (x[:n_p].reshape(bp, tp, d), x[n_p:].reshape(bs, ts, d),
            jnp.stack(hg_p), jnp.stack(hg_s), jnp.stack(rw_p), jnp.stack(rw_s),
            jnp.stack(sh_p), jnp.stack(sh_s), jnp.stack(pl_p), jnp.stack(pl_s))
```

```python
import functools

import jax
import jax.numpy as jnp
from jax import lax
from jax.experimental import pallas as pl
from jax.experimental.pallas import tpu as pltpu

F32 = jnp.float32
BF16 = jnp.bfloat16
HIGHEST = lax.Precision.HIGHEST

CHUNK_A = 16
POOL_WINDOWS = (2, 4, 8, 16)
TOP_K = 4
SWIGLU_LIMIT = 7.0
SWIGLU_ALPHA = 1.702
LN_EPS = 1e-5
RWKV_GN_EPS = 64e-5
PAST_LEN = 16384

VMEM_LIMIT_BYTES = 56 * 1024 * 1024
LANES = 128

MOE_SUPER = 1280
MOE_SUB = 640
MOE_TF = 256
MOE_TN = 256
ROUTER_TM = 512
COMBINE_TB = 128
RWKV_CHUNK = 64
DISPATCH_TB = 256
DMA_UNROLL = 8


def _cparams(sem):
    return pltpu.CompilerParams(dimension_semantics=sem, vmem_limit_bytes=VMEM_LIMIT_BYTES)


def _mm_kernel(x_ref, w_ref, o_ref):
    o_ref[...] = jnp.dot(x_ref[...].astype(BF16), w_ref[0].astype(BF16),
                         preferred_element_type=F32)


def matmul(x, w, layer, col0, n, tm=512, tn=1024):
    m, k = x.shape
    tm = min(tm, m)
    assert col0 % tn == 0
    c0 = col0 // tn
    return pl.pallas_call(
        _mm_kernel,
        grid=(pl.cdiv(n, tn), pl.cdiv(m, tm)),
        in_specs=[pl.BlockSpec((tm, k), lambda j, i: (i, 0)),
                  pl.BlockSpec((1, k, tn), lambda j, i: (layer, 0, c0 + j))],
        out_specs=pl.BlockSpec((tm, tn), lambda j, i: (i, j)),
        out_shape=jax.ShapeDtypeStruct((m, n), F32),
        compiler_params=_cparams(("arbitrary", "arbitrary")),
        name="matmul",
    )(x, w)


def _ln_rows(z, g, b):
    mu = jnp.mean(z, axis=-1, keepdims=True)
    zc = z - mu
    var = jnp.mean(zc * zc, axis=-1, keepdims=True)
    return zc * lax.rsqrt(var + LN_EPS) * g + b


def _proj_ln_kernel(*refs, n_in, alpha):
    x_ref = refs[0]
    a_refs = refs[1:1 + n_in]
    w_refs = refs[1 + n_in:1 + 2 * n_in]
    g_ref, b_ref, o_ref = refs[1 + 2 * n_in:]
    acc = alpha * x_ref[...]
    for a_ref, w_ref in zip(a_refs, w_refs):
        acc = acc + jnp.dot(a_ref[...].astype(BF16), w_ref[...], preferred_element_type=F32)
    o_ref[...] = _ln_rows(acc, g_ref[...], b_ref[...])


def proj_residual_ln(x, acts, weights, g, b, alpha, tm=256):
    m, d = x.shape
    n_in = len(acts)
    in_specs = [pl.BlockSpec((tm, d), lambda i: (i, 0))]
    in_specs += [pl.BlockSpec((tm, a.shape[1]), lambda i: (i, 0)) for a in acts]
    in_specs += [pl.BlockSpec(w.shape, lambda i: (0, 0)) for w in weights]
    in_specs += [pl.BlockSpec((1, d), lambda i: (0, 0))] * 2
    return pl.pallas_call(
        functools.partial(_proj_ln_kernel, n_in=n_in, alpha=alpha),
        grid=(m // tm,),
        in_specs=in_specs,
        out_specs=pl.BlockSpec((tm, d), lambda i: (i, 0)),
        out_shape=jax.ShapeDtypeStruct((m, d), F32),
        compiler_params=_cparams(("arbitrary",)),
        name="proj_residual_ln",
    )(x, *acts, *weights, g.reshape(1, d), b.reshape(1, d))


def _router_kernel(x_ref, w_ref, b_ref, ti_ref, tg_ref, cnt_ref, *, n_exp, top_k):
    i = pl.program_id(0)

    @pl.when(i == 0)
    def _():
        cnt_ref[...] = jnp.zeros_like(cnt_ref)

    logits = jnp.dot(x_ref[...].astype(BF16), w_ref[...].astype(BF16),
                     preferred_element_type=F32) + b_ref[...]
    tm = logits.shape[0]
    lane = lax.broadcasted_iota(jnp.int32, logits.shape, 1)
    work = logits
    firsts, vals, sel = [], [], jnp.zeros(logits.shape, F32)
    for _ in range(top_k):
        mx = jnp.max(work, axis=-1, keepdims=True)
        first = jnp.min(jnp.where(work == mx, lane, n_exp), axis=-1, keepdims=True)
        pick = lane == first
        firsts.append(first)
        vals.append(mx)
        sel = sel + pick.astype(F32)
        work = jnp.where(pick, -jnp.inf, work)
    exps = [jnp.exp(v - vals[0]) for v in vals]
    den = exps[0]
    for e in exps[1:]:
        den = den + e
    row = lax.broadcasted_iota(jnp.int32, (tm, tm), 0)
    col = lax.broadcasted_iota(jnp.int32, (tm, tm), 1)
    tril = jnp.where(row > col, 1.0, 0.0).astype(BF16)
    before = jnp.dot(tril, sel.astype(BF16), preferred_element_type=F32) + cnt_ref[...]
    out_lane = lax.broadcasted_iota(jnp.int32, (tm, LANES), 1)
    ti = jnp.zeros((tm, LANES), jnp.int32)
    tg = jnp.zeros((tm, LANES), F32)
    for k in range(top_k):
        rank_k = jnp.sum(jnp.where(lane == firsts[k], before, 0.0), axis=-1, keepdims=True)
        ti = jnp.where(out_lane == k, firsts[k], ti)
        ti = jnp.where(out_lane == top_k + k, rank_k.astype(jnp.int32), ti)
        tg = jnp.where(out_lane == k, exps[k] / den, tg)
    ti_ref[...] = ti
    tg_ref[...] = tg
    cnt_ref[...] = cnt_ref[...] + jnp.sum(sel, axis=0, keepdims=True)


def moe_router(x, w_router, b_router):
    n, d = x.shape
    n_exp = w_router.shape[1]
    tm = min(ROUTER_TM, n)
    return pl.pallas_call(
        functools.partial(_router_kernel, n_exp=n_exp, top_k=TOP_K),
        grid=(n // tm,),
        in_specs=[pl.BlockSpec((tm, d), lambda i: (i, 0)),
                  pl.BlockSpec((d, n_exp), lambda i: (0, 0)),
                  pl.BlockSpec((1, n_exp), lambda i: (0, 0))],
        out_specs=[pl.BlockSpec((tm, LANES), lambda i: (i, 0)),
                   pl.BlockSpec((tm, LANES), lambda i: (i, 0)),
                   pl.BlockSpec((1, n_exp), lambda i: (0, 0))],
        out_shape=[jax.ShapeDtypeStruct((n, LANES), jnp.int32),
                   jax.ShapeDtypeStruct((n, LANES), F32),
                   jax.ShapeDtypeStruct((1, n_exp), F32)],
        compiler_params=_cparams(("arbitrary",)),
        name="moe_router",
    )(x, w_router, b_router.reshape(1, n_exp))


def _dispatch_kernel(dest_ref, pad_start_ref, pad_cnt_ref, x_ref, xs_hbm, zero_ref, sem, zsem,
                     *, top_k, n_exp, tb):
    i = pl.program_id(0)
    base = i * (tb * top_k)

    def row_copy(r, k):
        return pltpu.make_async_copy(x_ref.at[pl.ds(r, 1)],
                                     xs_hbm.at[pl.ds(dest_ref[base + r * top_k + k], 1)], sem)

    def for_rows(fn):
        def body(g, c):
            for u in range(DMA_UNROLL):
                for k in range(top_k):
                    fn(row_copy(g * DMA_UNROLL + u, k))
            return c
        lax.fori_loop(0, tb // DMA_UNROLL, body, 0)

    for_rows(lambda cp: cp.start())

    @pl.when(i == 0)
    def _():
        zero_ref[...] = jnp.zeros_like(zero_ref)

        def pad_copy(e, r):
            return pltpu.make_async_copy(zero_ref.at[pl.ds(0, 1)],
                                         xs_hbm.at[pl.ds(pad_start_ref[e] + r, 1)], zsem)

        def pad_body(e, carry):
            def issue(r, cc):
                pad_copy(e, r).start()
                return cc
            lax.fori_loop(0, pad_cnt_ref[e], issue, 0)

            def drain(r, cc):
                pad_copy(e, r).wait()
                return cc
            lax.fori_loop(0, pad_cnt_ref[e], drain, 0)
            return carry

        lax.fori_loop(0, n_exp, pad_body, 0)

    for_rows(lambda cp: cp.wait())


def moe_dispatch(x, dest, pad_start, pad_cnt, n_rows):
    n, d = x.shape
    n_exp = pad_start.shape[0]
    tb = min(DISPATCH_TB, n)
    return pl.pallas_call(
        functools.partial(_dispatch_kernel, top_k=TOP_K, n_exp=n_exp, tb=tb),
        grid_spec=pltpu.PrefetchScalarGridSpec(
            num_scalar_prefetch=3,
            grid=(n // tb,),
            in_specs=[pl.BlockSpec((tb, d), lambda i, de, ps, pc: (i, 0))],
            out_specs=pl.BlockSpec(memory_space=pl.ANY),
            scratch_shapes=[pltpu.VMEM((8, d), F32),
                            pltpu.SemaphoreType.DMA(()),
                            pltpu.SemaphoreType.DMA(())],
        ),
        out_shape=jax.ShapeDtypeStruct((n_rows, d), F32),
        compiler_params=_cparams(("arbitrary",)),
        name="moe_dispatch",
    )(dest, pad_start, pad_cnt, x)


def _expert_kernel(blk_e_ref, blk_rows_ref, blk_src_ref,
                   x_ref, wg_ref, wu_ref, bg_ref, bu_ref, wd_ref, bd_ref, o_ref,
                   xb_s, h_s, wg_s, wu_s, wd_s, *, n_f):
    s = pl.program_id(0)
    t = pl.program_id(1)
    rows = blk_rows_ref[s]
    n_sub = MOE_SUPER // MOE_SUB

    def for_sub_blocks(fn):
        for r in range(n_sub):
            @pl.when(r * MOE_SUB < rows)
            def _():
                fn(pl.ds(r * MOE_SUB, MOE_SUB))

    @pl.when((rows > 0) & (t == 0))
    def _():
        def cast_x(sl):
            xb_s[sl, :] = x_ref[sl, :].astype(BF16)
        for_sub_blocks(cast_x)

    @pl.when((rows > 0) & (t < n_f))
    def _():
        wg_s[...] = wg_ref[0, 0].astype(BF16)
        wu_s[...] = wu_ref[0, 0].astype(BF16)

        def up_proj(sl):
            xb = xb_s[sl, :]
            hg = jnp.dot(xb, wg_s[...], preferred_element_type=F32) + bg_ref[0, 0]
            hu = jnp.dot(xb, wu_s[...], preferred_element_type=F32) + bu_ref[0, 0]
            gl = jnp.minimum(hg, SWIGLU_LIMIT)
            up = jnp.clip(hu, -SWIGLU_LIMIT, SWIGLU_LIMIT)
            h_s[t, sl, :] = (gl * jax.nn.sigmoid(SWIGLU_ALPHA * gl) * (up + 1.0)).astype(BF16)
        for_sub_blocks(up_proj)

    @pl.when((rows > 0) & (t >= n_f))
    def _():
        wd_s[...] = wd_ref[0, 0].astype(BF16)

        def down_proj(sl):
            h = jnp.concatenate([h_s[j, sl, :] for j in range(n_f)], axis=1)
            o_ref[sl, :] = jnp.dot(h, wd_s[...], preferred_element_type=F32) + bd_ref[0, 0]
        for_sub_blocks(down_proj)

        for r in range(1, n_sub):
            @pl.when(r * MOE_SUB >= rows)
            def _():
                o_ref[pl.ds(r * MOE_SUB, MOE_SUB), :] = jnp.zeros((MOE_SUB, o_ref.shape[1]), F32)


def moe_experts(xs, blk_e, blk_rows, blk_src, w_gu, b_gu, w_dn, b_dn, layer):
    n_rows, d = xs.shape
    n_layers, n_exp, _, two_f = w_gu.shape
    d_ff = two_f // 2
    n_super = n_rows // MOE_SUPER
    n_f = d_ff // MOE_TF
    n_d = d // MOE_TN

    def j1(s, t, rows):
        return jnp.where(rows[s] > 0, jnp.minimum(t, n_f - 1), n_f - 1)

    def j2(s, t, rows):
        return jnp.where(rows[s] > 0, jnp.maximum(t - n_f, 0), n_d - 1)

    return pl.pallas_call(
        functools.partial(_expert_kernel, n_f=n_f),
        grid_spec=pltpu.PrefetchScalarGridSpec(
            num_scalar_prefetch=3,
            grid=(n_super, n_f + n_d),
            in_specs=[
                pl.BlockSpec((MOE_SUPER, d), lambda s, t, be, br, bs: (bs[s], 0)),
                pl.BlockSpec((1, 1, d, MOE_TF),
                             lambda s, t, be, br, bs: (layer, be[s], 0, j1(s, t, br))),
                pl.BlockSpec((1, 1, d, MOE_TF),
                             lambda s, t, be, br, bs: (layer, be[s], 0, n_f + j1(s, t, br))),
                pl.BlockSpec((1, 1, 1, MOE_TF),
                             lambda s, t, be, br, bs: (layer, be[s], 0, j1(s, t, br))),
                pl.BlockSpec((1, 1, 1, MOE_TF),
                             lambda s, t, be, br, bs: (layer, be[s], 0, n_f + j1(s, t, br))),
                pl.BlockSpec((1, 1, d_ff, MOE_TN),
                             lambda s, t, be, br, bs: (layer, be[s], 0, j2(s, t, br))),
                pl.BlockSpec((1, 1, 1, MOE_TN),
                             lambda s, t, be, br, bs: (layer, be[s], 0, j2(s, t, br))),
            ],
            out_specs=pl.BlockSpec((MOE_SUPER, MOE_TN), lambda s, t, be, br, bs: (bs[s], j2(s, t, br))),
            scratch_shapes=[pltpu.VMEM((MOE_SUPER, d), BF16),
                            pltpu.VMEM((n_f, MOE_SUPER, MOE_TF), BF16),
                            pltpu.VMEM((d, MOE_TF), BF16),
                            pltpu.VMEM((d, MOE_TF), BF16),
                            pltpu.VMEM((d_ff, MOE_TN), BF16)],
        ),
        out_shape=jax.ShapeDtypeStruct((n_rows, d), F32),
        compiler_params=_cparams(("arbitrary", "arbitrary")),
        name="moe_experts",
    )(blk_e, blk_rows, blk_src, xs, w_gu, w_gu, b_gu.reshape(n_layers, n_exp, 1, two_f),
      b_gu.reshape(n_layers, n_exp, 1, two_f), w_dn, b_dn.reshape(n_layers, n_exp, 1, d))


def _combine_kernel(dest_ref, x_ref, tg_ref, g_ref, b_ref, yb_hbm, o_ref, buf, sem,
                    *, top_k, alpha, tb):
    i = pl.program_id(0)
    n_blk = pl.num_programs(0)

    def for_rows(blk, slot, fn):
        base = blk * (tb * top_k)

        def body(g, c):
            for u in range(DMA_UNROLL):
                r = g * DMA_UNROLL + u
                for k in range(top_k):
                    fn(pltpu.make_async_copy(
                        yb_hbm.at[pl.ds(dest_ref[base + r * top_k + k], 1)],
                        buf.at[slot, k, pl.ds(r, 1)], sem.at[slot]))
            return c
        lax.fori_loop(0, tb // DMA_UNROLL, body, 0)

    @pl.when(i == 0)
    def _():
        for_rows(0, 0, lambda cp: cp.start())

    @pl.when(i + 1 < n_blk)
    def _():
        for_rows(i + 1, (i + 1) % 2, lambda cp: cp.start())

    slot = i % 2
    for_rows(i, slot, lambda cp: cp.wait())

    tg = tg_ref[...]
    acc = alpha * x_ref[...]
    for k in range(top_k):
        acc = acc + tg[:, k:k + 1] * buf[slot, k]
    o_ref[...] = _ln_rows(acc, g_ref[...], b_ref[...])


def moe_combine_ln(x, yb, dest, tg, g, b, alpha):
    n, d = x.shape
    tb = COMBINE_TB
    return pl.pallas_call(
        functools.partial(_combine_kernel, top_k=TOP_K, alpha=alpha, tb=tb),
        grid_spec=pltpu.PrefetchScalarGridSpec(
            num_scalar_prefetch=1,
            grid=(n // tb,),
            in_specs=[pl.BlockSpec((tb, d), lambda i, dr: (i, 0)),
                      pl.BlockSpec((tb, LANES), lambda i, dr: (i, 0)),
                      pl.BlockSpec((1, d), lambda i, dr: (0, 0)),
                      pl.BlockSpec((1, d), lambda i, dr: (0, 0)),
                      pl.BlockSpec(memory_space=pl.ANY)],
            out_specs=pl.BlockSpec((tb, d), lambda i, dr: (i, 0)),
            scratch_shapes=[pltpu.VMEM((2, TOP_K, tb, d), F32),
                            pltpu.SemaphoreType.DMA((2,))],
        ),
        out_shape=jax.ShapeDtypeStruct((n, d), F32),
        compiler_params=_cparams(("arbitrary",)),
        name="moe_combine_ln",
    )(dest, x, tg, g.reshape(1, d), b.reshape(1, d), yb)


def moe_layer(x, w_router, b_router, w_gu, b_gu, w_dn, b_dn, layer, ln_g, ln_b, alpha):
    n, d = x.shape
    n_exp = w_router.shape[1]
    m = n * TOP_K
    n_super = -(-m // MOE_SUPER) + n_exp
    n_rows = n_super * MOE_SUPER

    ti, tg, cnt = moe_router(x, w_router, b_router)
    counts = cnt[0].astype(jnp.int32)
    nsup = (counts + MOE_SUPER - 1) // MOE_SUPER
    sup_end = jnp.cumsum(nsup)
    sup_start = sup_end - nsup
    row_start = sup_start * MOE_SUPER
    top_i = ti[:, :TOP_K]
    rank = ti[:, TOP_K:2 * TOP_K]
    dest = (row_start[top_i] + rank).reshape(m)
    s_idx = jnp.arange(n_super, dtype=jnp.int32)
    n_used = sup_end[-1]
    used = s_idx < n_used
    src = jnp.where(used, s_idx, n_used - 1)
    blk_e = jnp.minimum(jnp.searchsorted(sup_end, src, side="right"), n_exp - 1).astype(jnp.int32)
    blk_rows = jnp.clip(counts[blk_e] - (src - sup_start[blk_e]) * MOE_SUPER, 0, MOE_SUPER)
    blk_rows = jnp.where(used, blk_rows, 0).astype(jnp.int32)
    pad_start = row_start + counts
    pad_cnt = (-counts) % MOE_SUB

    xs = moe_dispatch(x, dest, pad_start, pad_cnt, n_rows)
    yb = moe_experts(xs, blk_e, blk_rows, src.astype(jnp.int32), w_gu, b_gu, w_dn, b_dn, layer)
    return moe_combine_ln(x, yb, dest, tg, ln_g, ln_b, alpha)


def _head_sum_matrix(n_b, scale):
    row = lax.broadcasted_iota(jnp.int32, (LANES, LANES), 0)
    col = lax.broadcasted_iota(jnp.int32, (LANES, LANES), 1)
    return jnp.where(row // n_b == col // n_b, scale, 0.0).astype(BF16)


def _split_dot(dot_fn, x, terms=3):
    acc = None
    for _ in range(terms):
        piece = x.astype(BF16)
        part = dot_fn(piece)
        acc = part if acc is None else acc + part
        x = x - piece.astype(F32)
    return acc


def _per_head(x, mat):
    outs = [_split_dot(lambda a: jnp.dot(a, mat, preferred_element_type=F32),
                       x[:, s * LANES:(s + 1) * LANES]) for s in range(x.shape[1] // LANES)]
    return jnp.concatenate(outs, axis=1)


def _rwkv_prep_kernel(p_ref, prev_ref, mu_ref, w0_ref, w2_ref, a0_ref, a2_ref, g2_ref, kk_ref,
                      ka_ref, r_ref, k_ref, v_ref, lw_ref, al_ref, be_ref, g_ref,
                      *, d_b, wl, al, n_b):
    p = p_ref[...]
    xm = p + (prev_ref[...] - p) * mu_ref[...]
    r = xm[:, :d_b]
    k = xm[:, d_b:2 * d_b]
    v = xm[:, 2 * d_b:3 * d_b]
    wd = xm[:, 3 * d_b:3 * d_b + wl]
    ad = xm[:, 3 * d_b + wl:3 * d_b + wl + al]
    gd = xm[:, 3 * d_b + wl + al:]
    dotb = lambda a, w_ref: jnp.dot(a.astype(BF16), w_ref[...].astype(BF16),
                                    preferred_element_type=F32)
    wlog = -jax.nn.softplus(-(w0_ref[...] + dotb(jnp.tanh(wd), w2_ref))) - 0.5
    a = jax.nn.sigmoid(a0_ref[...] + dotb(ad, a2_ref))
    kk = k * kk_ref[...]
    norm = jnp.sqrt(_per_head(kk * kk, _head_sum_matrix(n_b, 1.0)))
    kk = kk / jnp.maximum(norm, 1e-12)
    r_ref[...] = r
    k_ref[...] = k * (1.0 + (a - 1.0) * ka_ref[...])
    v_ref[...] = v
    lw_ref[...] = -jnp.exp(wlog)
    al_ref[...] = -kk
    be_ref[...] = kk * a
    g_ref[...] = dotb(jax.nn.sigmoid(gd), g2_ref)


def rwkv_prep(p, row0, prev, mu, w0, w2, a0, a2, g2, k_k, k_a, n_b, tm=256):
    n, pb = prev.shape
    blk0 = row0 // tm
    d_b = w0.shape[0]
    wl, al = w2.shape[0], a2.shape[0]
    row = lambda c: pl.BlockSpec((tm, c), lambda i: (i, 0))
    full = lambda a: pl.BlockSpec(a.shape, lambda i: (0, 0))
    vec = lambda a: a.reshape(1, -1)
    consts = [vec(mu), vec(w0), w2, vec(a0), a2, g2, vec(k_k), vec(k_a)]
    return pl.pallas_call(
        functools.partial(_rwkv_prep_kernel, d_b=d_b, wl=wl, al=al, n_b=n_b),
        grid=(n // tm,),
        in_specs=[pl.BlockSpec((tm, pb), lambda i: (blk0 + i, 0)), row(pb)] + [full(c) for c in consts],
        out_specs=[row(d_b)] * 7,
        out_shape=[jax.ShapeDtypeStruct((n, d_b), F32)] * 7,
        compiler_params=_cparams(("arbitrary",)),
        name="rwkv_prep",
    )(p, prev, *consts)


def _rwkv_chunk_kernel(r_ref, k_ref, v_ref, lw_ref, al_ref, be_ref, g_ref, s0_ref, gng_ref,
                       gnb_ref, rk_ref, o_ref, s_out_ref, st_ref, *, n_b, chunk, seg, n_pairs):
    c = pl.program_id(1)
    c2 = 2 * chunk
    n_seg = chunk // seg

    @pl.when(c == 0)
    def _():
        st_ref[...] = s0_ref[...]

    left = lambda mat: (lambda a: jnp.dot(mat, a, preferred_element_type=F32))
    right = lambda mat: (lambda a: jnp.dot(a, mat, preferred_element_type=F32))
    dot = lambda a, b: jnp.dot(a.astype(BF16), b.astype(BF16), preferred_element_type=F32)
    dot_nt = lambda a, b: lax.dot_general(a.astype(BF16), b.astype(BF16), (((1,), (1,)), ((), ())),
                                          preferred_element_type=F32)
    dot_tn = lambda a, b: lax.dot_general(a.astype(BF16), b.astype(BF16), (((0,), (0,)), ((), ())),
                                          preferred_element_type=F32)
    lane = lax.broadcasted_iota(jnp.int32, (chunk, LANES), 1)
    head_a = lane < n_b
    stack = lambda x: jnp.concatenate([jnp.where(head_a, x, 0.0), jnp.where(head_a, 0.0, x)], axis=0)
    twice = lambda x: jnp.concatenate([x, x], axis=0)
    row2 = lax.broadcasted_iota(jnp.int32, (c2, c2), 0)
    col2 = lax.broadcasted_iota(jnp.int32, (c2, c2), 1)
    same = (row2 // seg) == (col2 // seg)
    strict = same & (row2 > col2)
    incl = same & (row2 >= col2)
    rowc = lax.broadcasted_iota(jnp.int32, (chunk, chunk), 0)
    colc = lax.broadcasted_iota(jnp.int32, (chunk, chunk), 1)
    cum = jnp.where((rowc >= colc) & (rowc // seg == colc // seg), 1.0, 0.0).astype(BF16)
    rl = lax.broadcasted_iota(jnp.int32, (LANES, LANES), 0)
    cl_ = lax.broadcasted_iota(jnp.int32, (LANES, LANES), 1)
    blockdiag = (rl // n_b) == (cl_ // n_b)
    sum_mat = _head_sum_matrix(n_b, 1.0)
    segs = [slice(q * seg, (q + 1) * seg) for q in range(n_seg)]
    cat0 = lambda parts: parts[0] if len(parts) == 1 else jnp.concatenate(parts, axis=0)

    for p in range(n_pairs):
        sl = slice(p * LANES, (p + 1) * LANES)
        r, k, v = r_ref[:, sl], k_ref[:, sl], v_ref[:, sl]
        lw, al, be = lw_ref[:, sl], al_ref[:, sl], be_ref[:, sl]
        cl = _split_dot(left(cum), lw)
        e_neg = jnp.exp(-cl)
        at = al * jnp.exp(cl - lw)
        rt = r * jnp.exp(cl)
        bt = be * e_neg
        kt = k * e_neg
        gram = dot_nt(jnp.concatenate([stack(at), stack(rt)], axis=0),
                      jnp.concatenate([twice(bt), twice(kt)], axis=0))
        nm = jnp.where(strict, gram[:c2, :c2], 0.0)
        aak = jnp.where(strict, gram[:c2, c2:], 0.0)
        arb = jnp.where(incl, gram[c2:, :c2], 0.0)
        ark = jnp.where(incl, gram[c2:, c2:], 0.0)
        ms = [st_ref[q, p] for q in range(n_seg)]
        fs = [dot(jnp.concatenate([at[sq], rt[sq]], axis=0), ms[q]) for q, sq in enumerate(segs)]
        a_s0 = cat0([f[:seg] for f in fs])
        r_s0 = cat0([f[seg:] for f in fs])
        vs = stack(v)
        u = stack(a_s0) + dot(aak, vs)
        pw = nm
        n = 1
        while n < seg:
            u = u + dot(pw, u)
            n *= 2
            if n < seg:
                pw = dot(pw, pw)
        ys = stack(r_s0) + dot(jnp.concatenate([arb, ark], axis=1), jnp.concatenate([u, vs], axis=0))
        y = ys[:chunk] + ys[chunk:]
        up = u[:chunk] + u[chunk:]
        ends = [cl[sq.stop - 1:sq.stop, :] for sq in segs]
        to_end = jnp.exp(cat0([jnp.broadcast_to(e, (seg, LANES)) for e in ends]) - cl)
        b_end, k_end = be * to_end, k * to_end
        for q, sq in enumerate(segs):
            upd = dot_tn(jnp.concatenate([b_end[sq], k_end[sq]], axis=0),
                         jnp.concatenate([up[sq], v[sq]], axis=0))
            pc_col = jnp.transpose(jnp.broadcast_to(jnp.exp(ends[q]), (8, LANES)))[:, 0:1]
            st_ref[q, p] = pc_col * ms[q] + jnp.where(blockdiag, upd, 0.0)

        stats = _split_dot(right(sum_mat), jnp.concatenate([y, y * y, r * k * rk_ref[:, sl]], axis=0))
        mean = stats[:chunk] * (1.0 / n_b)
        var = stats[chunk:c2] * (1.0 / n_b) - mean * mean
        yn = (y - mean) * lax.rsqrt(var + RWKV_GN_EPS) * gng_ref[:, sl] + gnb_ref[:, sl]
        o_ref[:, sl] = (yn + stats[c2:] * v) * g_ref[:, sl]

    @pl.when(c == pl.num_programs(1) - 1)
    def _():
        s_out_ref[...] = st_ref[...]


def rwkv_chunked(r, k, v, lw, al, be, g, s0_bd, gn_g, gn_b, r_k, bsz, t, n_b, chunk):
    n, d_b = r.shape
    n_pairs = d_b // LANES
    seg = min(t, chunk)
    n_seg = chunk // seg
    n_c = t // seg
    row = pl.BlockSpec((chunk, d_b), lambda b, c: (b * n_c + c, 0))
    vec = pl.BlockSpec((1, d_b), lambda b, c: (0, 0))
    st = pl.BlockSpec((n_seg, n_pairs, LANES, LANES), lambda b, c: (b, 0, 0, 0))
    return pl.pallas_call(
        functools.partial(_rwkv_chunk_kernel, n_b=n_b, chunk=chunk, seg=seg, n_pairs=n_pairs),
        grid=(bsz // n_seg, n_c),
        in_specs=[row] * 7 + [st, vec, vec, vec],
        out_specs=[row, st],
        out_shape=[jax.ShapeDtypeStruct((n, d_b), F32),
                   jax.ShapeDtypeStruct(s0_bd.shape, F32)],
        scratch_shapes=[pltpu.VMEM((n_seg, n_pairs, LANES, LANES), F32)],
        compiler_params=_cparams(("arbitrary", "arbitrary")),
        name="rwkv_chunked",
    )(r, k, v, lw, al, be, g, s0_bd, gn_g.reshape(1, d_b), gn_b.reshape(1, d_b),
      r_k.reshape(1, d_b))


def _state_to_blockdiag(s):
    bsz, n_h, n_v, n_k = s.shape
    st = jnp.swapaxes(s, 2, 3).reshape(bsz, n_h // 2, 2, n_k, n_v)
    z = jnp.zeros_like(st[:, :, 0])
    top = jnp.concatenate([st[:, :, 0], z], axis=-1)
    bot = jnp.concatenate([z, st[:, :, 1]], axis=-1)
    return jnp.concatenate([top, bot], axis=-2)


def _blockdiag_to_state(m, n_b):
    bsz, n_pairs = m.shape[:2]
    a = m[:, :, :n_b, :n_b]
    b = m[:, :, n_b:, n_b:]
    st = jnp.stack([a, b], axis=2).reshape(bsz, n_pairs * 2, n_b, n_b)
    return jnp.swapaxes(st, 2, 3)


def rwkv7_group(p_all, row0, bsz, t, shift_prev, s0, mu, w0, w2, a0, a2, g2, k_k, k_a, r_k, gn_g,
                gn_b, chunk):
    pb = p_all.shape[1]
    n_b = s0.shape[2]
    p = p_all[row0:row0 + bsz * t].reshape(bsz, t, pb)
    prev = jnp.concatenate([shift_prev[:, None], p[:, :-1]], axis=1).reshape(bsz * t, pb)
    tm = 256
    r, k, v, lw, al, be, g = rwkv_prep(p_all, row0, prev, mu, w0, w2, a0, a2, g2, k_k, k_a, n_b, tm=tm)
    o, m = rwkv_chunked(r, k, v, lw, al, be, g, _state_to_blockdiag(s0), gn_g, gn_b, r_k,
                        bsz, t, n_b, chunk)
    return o, _blockdiag_to_state(m, n_b), p[:, -1]


def _hgrn_kernel(q_ref, f_ref, i_ref, gate_ref, lb_ref, ng_ref, s0_ref, o_ref, s_out_ref, st_ref,
                 *, n_h, sub, tb):
    c = pl.program_id(1)

    @pl.when(c == 0)
    def _():
        st_ref[...] = s0_ref[0]

    row = lax.broadcasted_iota(jnp.int32, (sub, sub), 0)
    col = lax.broadcasted_iota(jnp.int32, (sub, sub), 1)
    cum = jnp.where(row >= col, 1.0, 0.0).astype(BF16)
    t_idx = lax.broadcasted_iota(jnp.int32, (sub, LANES), 0)
    bf = lambda x: x.astype(BF16)

    def body(j, carry):
        rows = pl.ds(pl.multiple_of(j * sub, sub), sub)
        for h in range(n_h):
            sl = slice(h * LANES, (h + 1) * LANES)
            lb = lb_ref[:, sl]
            s_mat = st_ref[h]
            q = jax.nn.silu(q_ref[rows, sl])
            z = f_ref[rows, sl]
            logf = jnp.log(lb + (1.0 - lb) * jax.nn.sigmoid(z))
            kk = (1.0 - lb) * jax.nn.sigmoid(-z)
            v = i_ref[rows, sl]
            b = _split_dot(lambda a: jnp.dot(cum, a, preferred_element_type=F32), logf)
            o = jnp.dot(bf(q * jnp.exp(b)), bf(s_mat), preferred_element_type=F32)
            vb = bf(v).astype(F32)
            for s in range(sub):
                dec = jnp.exp(jnp.where(t_idx >= s, b - b[s:s + 1, :], -jnp.inf))
                att = jnp.sum(q * kk[s:s + 1, :] * dec, axis=-1, keepdims=True)
                o = o + bf(att).astype(F32) * vb[s:s + 1, :]
            bl = b[sub - 1:sub, :]
            kd = bf(kk * jnp.exp(bl - b))
            upd = lax.dot_general(kd, bf(v), (((0,), (0,)), ((), ())), preferred_element_type=F32)
            decay = jnp.exp(bl)
            st_ref[h] = jnp.transpose(jnp.broadcast_to(decay, (8, LANES)))[:, 0:1] * s_mat + upd
            o = o * lax.rsqrt(jnp.mean(o * o, axis=-1, keepdims=True) + LN_EPS) * ng_ref[...]
            o_ref[rows, sl] = o * jax.nn.silu(gate_ref[rows, sl])
        return carry

    lax.fori_loop(0, tb // sub, body, 0)

    @pl.when(c == pl.num_programs(1) - 1)
    def _():
        s_out_ref[0] = st_ref[...]


def hgrn2(proj, row0, s0, lb, norm_g, bsz, t):
    n = bsz * t
    n_h, dk, dv = s0.shape[1:]
    d_a = n_h * dk
    sub = min(CHUNK_A, t)
    tb = min(256, t)
    n_c = t // tb
    blk0 = row0 // tb
    col = lambda which: pl.BlockSpec((tb, d_a), lambda b, c: (blk0 + b * n_c + c, which))
    st = pl.BlockSpec((1, n_h, dk, dv), lambda b, c: (b, 0, 0, 0))
    return pl.pallas_call(
        functools.partial(_hgrn_kernel, n_h=n_h, sub=sub, tb=tb),
        grid=(bsz, n_c),
        in_specs=[col(0), col(1), col(2), col(3),
                  pl.BlockSpec((1, d_a), lambda b, c: (0, 0)),
                  pl.BlockSpec((1, dv), lambda b, c: (0, 0)), st],
        out_specs=[pl.BlockSpec((tb, d_a), lambda b, c: (b * n_c + c, 0)), st],
        out_shape=[jax.ShapeDtypeStruct((n, d_a), F32), jax.ShapeDtypeStruct(s0.shape, F32)],
        scratch_shapes=[pltpu.VMEM((n_h, dk, dv), F32)],
        compiler_params=_cparams(("arbitrary", "arbitrary")),
        name="hgrn2",
    )(proj, proj, proj, proj, lb.reshape(1, d_a), norm_g.reshape(1, dv), s0)


def _pool_ln_kernel(x_ref, halo_ref, w_ref, sc_ref, g_ref, b_ref, o_ref, xx_ref,
                    *, windows, start_pos, alpha, tb, halo):
    c = pl.program_id(1)
    d = x_ref.shape[1]
    pc = d // len(windows)
    xx_ref[0:halo, :] = halo_ref[0, 0]
    xx_ref[halo:halo + tb, :] = x_ref[...]
    x = x_ref[...]
    pos = start_pos + c * tb + lax.broadcasted_iota(jnp.int32, (tb, 1), 0)
    ys = []
    for gi, w in enumerate(windows):
        sl = slice(gi * pc, (gi + 1) * pc)
        acc = x[:, sl]
        for back in range(1, w):
            acc = acc + xx_ref[halo - back:halo - back + tb, sl]
        cnt = jnp.minimum(pos + 1, w).astype(F32)
        u = acc / cnt - x[:, sl]
        ys.append(jnp.dot(u.astype(BF16), w_ref[gi], preferred_element_type=F32))
    y = jnp.concatenate(ys, axis=1) * sc_ref[...]
    o_ref[...] = _ln_rows(alpha * x + y, g_ref[...], b_ref[...])


def pool_residual_ln(x, hist, start_pos, w_pool, scale, g, b, alpha, bsz, t):
    n, d = x.shape
    halo = 16
    tb = min(256, t)
    n_c = t // tb
    x4 = x.reshape(bsz, n_c, tb, d)
    first = jnp.concatenate([jnp.zeros((bsz, halo - hist.shape[1], d), F32), hist], axis=1)
    halos = first[:, None]
    if n_c > 1:
        halos = jnp.concatenate([halos, x4[:, :-1, tb - halo:]], axis=1)
    vec = lambda a: pl.BlockSpec((1, d), lambda bi, c: (0, 0))
    return pl.pallas_call(
        functools.partial(_pool_ln_kernel, windows=POOL_WINDOWS, start_pos=start_pos,
                          alpha=alpha, tb=tb, halo=halo),
        grid=(bsz, n_c),
        in_specs=[pl.BlockSpec((tb, d), lambda bi, c: (bi * n_c + c, 0)),
                  pl.BlockSpec((1, 1, halo, d), lambda bi, c: (bi, c, 0, 0)),
                  pl.BlockSpec(w_pool.shape, lambda bi, c: (0, 0, 0)),
                  vec(scale), vec(g), vec(b)],
        out_specs=pl.BlockSpec((tb, d), lambda bi, c: (bi * n_c + c, 0)),
        out_shape=jax.ShapeDtypeStruct((n, d), F32),
        scratch_shapes=[pltpu.VMEM((halo + tb, d), F32)],
        compiler_params=_cparams(("arbitrary", "arbitrary")),
        name="pool_residual_ln",
    )(x, halos, w_pool.astype(BF16), scale.reshape(1, d), g.reshape(1, d), b.reshape(1, d))


def _hgrn2_group(q, f, i, gate, s0, lb, norm_g):
    bsz, t, d_a = q.shape
    n_h, dk, dv = s0.shape[1:]
    q = jax.nn.silu(q)
    logf = jnp.log(lb + (1.0 - lb) * jax.nn.sigmoid(f))
    k = (1.0 - lb) * jax.nn.sigmoid(-f)
    tp = -(-t // CHUNK_A) * CHUNK_A
    pad = [(0, 0), (0, tp - t), (0, 0)]
    q, k, v, logf = [jnp.pad(a, pad) for a in (q, k, i, logf)]
    nc = tp // CHUNK_A
    blocks = lambda a, dd: a.reshape(bsz, nc, CHUNK_A, n_h, dd).transpose(1, 0, 3, 2, 4)
    mask = jnp.tril(jnp.ones((CHUNK_A, CHUNK_A), bool))

    def step(S, inp):
        qc, kc, vc, gc = inp
        b = jnp.cumsum(gc, axis=2)
        diff = b[:, :, :, None, :] - b[:, :, None, :, :]
        dec = jnp.exp(jnp.where(mask[:, :, None], diff, -jnp.inf))
        att = jnp.sum(qc[:, :, :, None, :] * kc[:, :, None, :, :] * dec, axis=-1)
        o = (jnp.einsum('bhts,bhsv->bhtv', att, vc)
             + jnp.einsum('bhtd,bhdv->bhtv', qc * jnp.exp(b), S))
        bl = b[:, :, -1:, :]
        S = (jnp.exp(bl[:, :, 0, :])[..., None] * S
             + jnp.einsum('bhsd,bhsv->bhdv', kc * jnp.exp(bl - b), vc))
        return S, o

    s_t, o = lax.scan(step, s0, (blocks(q, dk), blocks(k, dk), blocks(v, dv), blocks(logf, dk)))
    o = o.transpose(1, 0, 3, 2, 4).reshape(bsz, tp, n_h, dv)[:, :t]
    o = o * lax.rsqrt(jnp.mean(o * o, -1, keepdims=True) + LN_EPS) * norm_g
    return o.reshape(bsz, t, d_a) * jax.nn.silu(gate), s_t


def _rwkv7_group(p, shift_prev, s0, mu, w0, w2, a0, a2, g2, k_k, k_a, r_k, gn_g, gn_b, chunk):
    bsz, t, _ = p.shape
    d_b = w0.shape[0]
    n_h, n_b = s0.shape[1], s0.shape[2]
    wl, al_ = w2.shape[0], a2.shape[0]
    prev = jnp.concatenate([shift_prev[:, None], p[:, :-1]], axis=1)
    xm = p + (prev - p) * mu
    r, k, v, wd, ad, gd = jnp.split(
        xm, [d_b, 2 * d_b, 3 * d_b, 3 * d_b + wl, 3 * d_b + wl + al_], axis=-1)
    wlog = -jax.nn.softplus(-(w0 + jnp.tanh(wd) @ w2)) - 0.5
    logw = -jnp.exp(wlog)
    a = jax.nn.sigmoid(a0 + ad @ a2)
    g = jax.nn.sigmoid(gd) @ g2
    hs = lambda x: x.reshape(bsz, t, n_h, n_b)
    r, k, v, logw, a = map(hs, (r, k, v, logw, a))
    kk = k * k_k.reshape(n_h, n_b)
    kk = kk / jnp.maximum(jnp.sqrt(jnp.sum(kk * kk, -1, keepdims=True)), 1e-12)
    k2 = k * (1.0 + (a - 1.0) * k_a.reshape(n_h, n_b))
    al = -kk
    be = kk * a
    c = chunk
    nc = t // c
    ch = lambda x: x.reshape(bsz, nc, c, n_h, n_b).transpose(1, 0, 3, 2, 4)
    strict = jnp.tril(jnp.ones((c, c), bool), -1)
    incl = jnp.tril(jnp.ones((c, c), bool))
    ein = functools.partial(jnp.einsum, precision=HIGHEST)

    def step(S, inp):
        rc, kc, vc, lc, ac, bc = inp
        cl = jnp.cumsum(lc, axis=2)
        at = ac * jnp.exp(cl - lc)
        rt = rc * jnp.exp(cl)
        bt = bc * jnp.exp(-cl)
        kt = kc * jnp.exp(-cl)
        nm = jnp.where(strict, ein('bhck,bhik->bhci', at, bt), 0.0)
        aak = jnp.where(strict, ein('bhck,bhik->bhci', at, kt), 0.0)
        arb = jnp.where(incl, ein('bhck,bhik->bhci', rt, bt), 0.0)
        ark = jnp.where(incl, ein('bhck,bhik->bhci', rt, kt), 0.0)
        u = ein('bhck,bhvk->bhcv', at, S) + ein('bhci,bhiv->bhcv', aak, vc)
        pw = nm
        n = 1
        while n < c:
            u = u + ein('bhci,bhiv->bhcv', pw, u)
            n *= 2
            if n < c:
                pw = ein('bhci,bhij->bhcj', pw, pw)
        y = (ein('bhck,bhvk->bhcv', rt, S) + ein('bhci,bhiv->bhcv', arb, u)
             + ein('bhci,bhiv->bhcv', ark, vc))
        sn = jnp.exp(cl[:, :, -1])[:, :, None, :] * (
            S + ein('bhiv,bhik->bhvk', u, bt) + ein('bhiv,bhik->bhvk', vc, kt))
        return sn, y

    s_t, y = lax.scan(step, s0, tuple(map(ch, (r, k2, v, logw, al, be))))
    y = y.transpose(1, 0, 3, 2, 4).reshape(bsz, t, n_h, n_b)
    mean = jnp.mean(y, -1, keepdims=True)
    var = jnp.mean(jnp.square(y - mean), -1, keepdims=True)
    y = (y - mean) * lax.rsqrt(var + RWKV_GN_EPS) * gn_g.reshape(n_h, n_b) + gn_b.reshape(n_h, n_b)
    y = y + jnp.sum(r * k2 * r_k, -1, keepdims=True) * v
    return y.reshape(bsz, t, d_b) * g, s_t, p[:, -1]


def _pool_group(x, hist, start_pos, w_pool, scale):
    bsz, t, d = x.shape
    n_g = len(POOL_WINDOWS)
    pc = d // n_g
    hlen = hist.shape[1]
    xx = jnp.concatenate([hist, x], axis=1)
    pos = jnp.arange(t) + start_pos
    outs = []
    for gi, w in enumerate(POOL_WINDOWS):
        sl = slice(gi * pc, (gi + 1) * pc)
        acc = x[..., sl]
        for back in range(1, w):
            acc = acc + xx[:, hlen - back:hlen - back + t, sl]
        cnt = jnp.minimum(pos + 1, w).astype(F32)[None, :, None]
        outs.append(acc / cnt - x[..., sl])
    u = jnp.stack(outs, axis=2)
    y = jnp.einsum('btgc,gcd->btgd', u, w_pool).reshape(bsz, t, d) * scale
    return y, xx[:, -hlen:]


def kernel(x_prompt, x_sample, state_hgrn, state_rwkv, state_rwkv_shift, state_pool, mix_w_in, hgrn_lb, hgrn_norm_g, rwkv_mu, rwkv_w0, rwkv_w2, rwkv_a0, rwkv_a2, rwkv_g2, rwkv_k_k, rwkv_k_a, rwkv_r_k, rwkv_gn_g, rwkv_gn_b, mix_w_out, pool_w, pool_scale, ln1_g, ln1_b, ln2_g, ln2_b, moe_w_router, moe_b_router, moe_w_gu, moe_b_gu, moe_w_dn, moe_b_dn):
    bp, tp, d = x_prompt.shape
    bs, ts, _ = x_sample.shape
    depth = ln1_g.shape[0]
    n_h_a, dk_a, dv_a = state_hgrn.shape[2:]
    n_h_b, n_b = state_rwkv.shape[2:4]
    d_a = n_h_a * dk_a
    pb = state_rwkv_shift.shape[2]
    hist = state_pool.shape[2]
    n_p = bp * tp
    alpha = (2 * depth) ** 0.25

    lb_p = jax.nn.softmax(hgrn_lb, axis=0)
    lb_all = jnp.cumsum(lb_p, axis=0) - lb_p[0]
    x = jnp.concatenate([x_prompt.reshape(n_p, d), x_sample.reshape(bs * ts, d)], axis=0)
    hg_p, hg_s, rw_p, rw_s, sh_p, sh_s, pl_p, pl_s = [], [], [], [], [], [], [], []
    for l in range(depth):
        if l % 2 == 0:
            e = l // 2
            p_a = matmul(x, mix_w_in, e, 0, 4 * d_a)
            p_b = matmul(x, mix_w_in, e, 4 * d_a, pb)
            rw = (rwkv_mu[e], rwkv_w0[e], rwkv_w2[e], rwkv_a0[e], rwkv_a2[e], rwkv_g2[e],
                  rwkv_k_k[e], rwkv_k_a[e], rwkv_r_k[e], rwkv_gn_g[e], rwkv_gn_b[e])
            outs = []
            for grp, (lo, hi, bsz, t) in enumerate(((0, n_p, bp, tp), (n_p, n_p + bs * ts, bs, ts))):
                if grp == 0:
                    s_h = jnp.zeros((bsz, n_h_a, dk_a, dv_a), F32)
                    s_r = jnp.zeros((bsz, n_h_b, n_b, n_b), F32)
                    s_s = jnp.zeros((bsz, pb), F32)
                else:
                    s_h, s_r, s_s = state_hgrn[e], state_rwkv[e], state_rwkv_shift[e]
                o_a, n_h = hgrn2(p_a, lo, s_h, lb_all[e], hgrn_norm_g[e], bsz, t)
                o_b, n_r, n_s = rwkv7_group(p_b, lo, bsz, t, s_s, s_r, *rw, chunk=RWKV_CHUNK)
                outs.append((o_a, o_b))
                (hg_p, hg_s)[grp].append(n_h)
                (rw_p, rw_s)[grp].append(n_r)
                (sh_p, sh_s)[grp].append(n_s)
            o_a = jnp.concatenate([outs[0][0], outs[1][0]], axis=0)
            o_b = jnp.concatenate([outs[0][1], outs[1][1]], axis=0)
            w_out = mix_w_out[e].astype(BF16)
            x = proj_residual_ln(x, [o_a, o_b], [w_out[:d_a], w_out[d_a:]], ln1_g[l], ln1_b[l], alpha)
        else:
            j = l // 2
            x_p, x_s = x[:n_p], x[n_p:]
            hist_p = jnp.zeros((bp, hist, d), F32)
            pl_p.append(jnp.concatenate([hist_p, x_p.reshape(bp, tp, d)], axis=1)[:, -hist:])
            pl_s.append(jnp.concatenate([state_pool[j], x_s.reshape(bs, ts, d)], axis=1)[:, -hist:])
            pool = (pool_w[j], pool_scale[j], ln1_g[l], ln1_b[l], alpha)
            x = jnp.concatenate([pool_residual_ln(x_p, hist_p, 0, *pool, bp, tp),
                                 pool_residual_ln(x_s, state_pool[j], PAST_LEN, *pool, bs, ts)], axis=0)
        x = moe_layer(x, moe_w_router[l], moe_b_router[l], moe_w_gu, moe_b_gu, moe_w_dn, moe_b_dn, l,
                      ln2_g[l], ln2_b[l], alpha)
    return (x[:n_p].reshape(bp, tp, d), x[n_p:].reshape(bs, ts, d),
            jnp.stack(hg_p), jnp.stack(hg_s), jnp.stack(rw_p), jnp.stack(rw_s),
            jnp.stack(sh_p), jnp.stack(sh_s), jnp.stack(pl_p), jnp.stack(pl_s))
```

```python
import functools

import jax
import jax.numpy as jnp
from jax import lax
from jax.experimental import pallas as pl
from jax.experimental.pallas import tpu as pltpu

F32 = jnp.float32
BF16 = jnp.bfloat16
HIGHEST = lax.Precision.HIGHEST

CHUNK_A = 16
POOL_WINDOWS = (2, 4, 8, 16)
TOP_K = 4
SWIGLU_LIMIT = 7.0
SWIGLU_ALPHA = 1.702
LN_EPS = 1e-5
RWKV_GN_EPS = 64e-5
PAST_LEN = 16384

VMEM_LIMIT_BYTES = 56 * 1024 * 1024
LANES = 128

MOE_SUPER = 1280
MOE_SUB = 640
MOE_TF = 256
MOE_TN = 256
ROUTER_TM = 512
COMBINE_TB = 128
HGRN_HEAD_GROUP = 8
RWKV_CHUNK = 64
DISPATCH_TB = 256
DMA_UNROLL = 8


def _cparams(sem):
    return pltpu.CompilerParams(dimension_semantics=sem, vmem_limit_bytes=VMEM_LIMIT_BYTES)


def _shared_rows_call(kernel, n_in, out_buf, **kwargs):
    if out_buf is None:
        return pl.pallas_call(kernel, **kwargs)
    body = lambda *refs: kernel(*refs[:n_in], *refs[n_in + 1:])
    kwargs["in_specs"] = list(kwargs["in_specs"]) + [pl.BlockSpec(memory_space=pl.ANY)]
    call = pl.pallas_call(body, input_output_aliases={n_in: 0}, **kwargs)
    return lambda *args: call(*args, out_buf)


def _mm_kernel(x_ref, w_ref, o_ref):
    o_ref[...] = jnp.dot(x_ref[...].astype(BF16), w_ref[0].astype(BF16),
                         preferred_element_type=F32)


def matmul(x, w, layer, col0, n, tm=512, tn=1024):
    m, k = x.shape
    tm = min(tm, m)
    assert col0 % tn == 0
    c0 = col0 // tn
    return pl.pallas_call(
        _mm_kernel,
        grid=(pl.cdiv(n, tn), pl.cdiv(m, tm)),
        in_specs=[pl.BlockSpec((tm, k), lambda j, i: (i, 0)),
                  pl.BlockSpec((1, k, tn), lambda j, i: (layer, 0, c0 + j))],
        out_specs=pl.BlockSpec((tm, tn), lambda j, i: (i, j)),
        out_shape=jax.ShapeDtypeStruct((m, n), F32),
        compiler_params=_cparams(("arbitrary", "arbitrary")),
        name="matmul",
    )(x, w)


def _ln_rows(z, g, b):
    mu = jnp.mean(z, axis=-1, keepdims=True)
    zc = z - mu
    var = jnp.mean(zc * zc, axis=-1, keepdims=True)
    return zc * lax.rsqrt(var + LN_EPS) * g + b


def _proj_ln_kernel(*refs, n_in, alpha):
    x_ref = refs[0]
    a_refs = refs[1:1 + n_in]
    w_refs = refs[1 + n_in:1 + 2 * n_in]
    g_ref, b_ref, o_ref = refs[1 + 2 * n_in:]
    acc = alpha * x_ref[...]
    for a_ref, w_ref in zip(a_refs, w_refs):
        acc = acc + jnp.dot(a_ref[...].astype(BF16), w_ref[...], preferred_element_type=F32)
    o_ref[...] = _ln_rows(acc, g_ref[...], b_ref[...])


def proj_residual_ln(x, acts, weights, g, b, alpha, tm=256):
    m, d = x.shape
    n_in = len(acts)
    in_specs = [pl.BlockSpec((tm, d), lambda i: (i, 0))]
    in_specs += [pl.BlockSpec((tm, a.shape[1]), lambda i: (i, 0)) for a in acts]
    in_specs += [pl.BlockSpec(w.shape, lambda i: (0, 0)) for w in weights]
    in_specs += [pl.BlockSpec((1, d), lambda i: (0, 0))] * 2
    return pl.pallas_call(
        functools.partial(_proj_ln_kernel, n_in=n_in, alpha=alpha),
        grid=(m // tm,),
        in_specs=in_specs,
        out_specs=pl.BlockSpec((tm, d), lambda i: (i, 0)),
        out_shape=jax.ShapeDtypeStruct((m, d), F32),
        compiler_params=_cparams(("arbitrary",)),
        name="proj_residual_ln",
    )(x, *acts, *weights, g.reshape(1, d), b.reshape(1, d))


def _router_kernel(x_ref, w_ref, b_ref, ti_ref, tg_ref, cnt_ref, *, n_exp, top_k):
    i = pl.program_id(0)

    @pl.when(i == 0)
    def _():
        cnt_ref[...] = jnp.zeros_like(cnt_ref)

    logits = jnp.dot(x_ref[...].astype(BF16), w_ref[...].astype(BF16),
                     preferred_element_type=F32) + b_ref[...]
    tm = logits.shape[0]
    lane = lax.broadcasted_iota(jnp.int32, logits.shape, 1)
    work = logits
    firsts, vals, sel = [], [], jnp.zeros(logits.shape, F32)
    for _ in range(top_k):
        mx = jnp.max(work, axis=-1, keepdims=True)
        first = jnp.min(jnp.where(work == mx, lane, n_exp), axis=-1, keepdims=True)
        pick = lane == first
        firsts.append(first)
        vals.append(mx)
        sel = sel + pick.astype(F32)
        work = jnp.where(pick, -jnp.inf, work)
    exps = [jnp.exp(v - vals[0]) for v in vals]
    den = exps[0]
    for e in exps[1:]:
        den = den + e
    row = lax.broadcasted_iota(jnp.int32, (tm, tm), 0)
    col = lax.broadcasted_iota(jnp.int32, (tm, tm), 1)
    tril = jnp.where(row > col, 1.0, 0.0).astype(BF16)
    before = jnp.dot(tril, sel.astype(BF16), preferred_element_type=F32) + cnt_ref[...]
    out_lane = lax.broadcasted_iota(jnp.int32, (tm, LANES), 1)
    ti = jnp.zeros((tm, LANES), jnp.int32)
    tg = jnp.zeros((tm, LANES), F32)
    for k in range(top_k):
        rank_k = jnp.sum(jnp.where(lane == firsts[k], before, 0.0), axis=-1, keepdims=True)
        ti = jnp.where(out_lane == k, firsts[k], ti)
        ti = jnp.where(out_lane == top_k + k, rank_k.astype(jnp.int32), ti)
        tg = jnp.where(out_lane == k, exps[k] / den, tg)
    ti_ref[...] = ti
    tg_ref[...] = tg
    cnt_ref[...] = cnt_ref[...] + jnp.sum(sel, axis=0, keepdims=True)


def moe_router(x, w_router, b_router):
    n, d = x.shape
    n_exp = w_router.shape[1]
    tm = min(ROUTER_TM, n)
    return pl.pallas_call(
        functools.partial(_router_kernel, n_exp=n_exp, top_k=TOP_K),
        grid=(n // tm,),
        in_specs=[pl.BlockSpec((tm, d), lambda i: (i, 0)),
                  pl.BlockSpec((d, n_exp), lambda i: (0, 0)),
                  pl.BlockSpec((1, n_exp), lambda i: (0, 0))],
        out_specs=[pl.BlockSpec((tm, LANES), lambda i: (i, 0)),
                   pl.BlockSpec((tm, LANES), lambda i: (i, 0)),
                   pl.BlockSpec((1, n_exp), lambda i: (0, 0))],
        out_shape=[jax.ShapeDtypeStruct((n, LANES), jnp.int32),
                   jax.ShapeDtypeStruct((n, LANES), F32),
                   jax.ShapeDtypeStruct((1, n_exp), F32)],
        compiler_params=_cparams(("arbitrary",)),
        name="moe_router",
    )(x, w_router, b_router.reshape(1, n_exp))


def _dispatch_kernel(dest_ref, pad_start_ref, pad_cnt_ref, x_ref, xs_hbm, zero_ref, sem, zsem,
                     *, top_k, n_exp, tb):
    i = pl.program_id(0)
    base = i * (tb * top_k)

    def row_copy(r, k):
        return pltpu.make_async_copy(x_ref.at[pl.ds(r, 1)],
                                     xs_hbm.at[pl.ds(dest_ref[base + r * top_k + k], 1)], sem)

    def for_rows(fn):
        def body(g, c):
            for u in range(DMA_UNROLL):
                for k in range(top_k):
                    fn(row_copy(g * DMA_UNROLL + u, k))
            return c
        lax.fori_loop(0, tb // DMA_UNROLL, body, 0)

    for_rows(lambda cp: cp.start())

    @pl.when(i == 0)
    def _():
        zero_ref[...] = jnp.zeros_like(zero_ref)

        def pad_copy(e, r):
            return pltpu.make_async_copy(zero_ref.at[pl.ds(0, 1)],
                                         xs_hbm.at[pl.ds(pad_start_ref[e] + r, 1)], zsem)

        def pad_body(e, carry):
            def issue(r, cc):
                pad_copy(e, r).start()
                return cc
            lax.fori_loop(0, pad_cnt_ref[e], issue, 0)

            def drain(r, cc):
                pad_copy(e, r).wait()
                return cc
            lax.fori_loop(0, pad_cnt_ref[e], drain, 0)
            return carry

        lax.fori_loop(0, n_exp, pad_body, 0)

    pltpu.make_async_copy(xs_hbm.at[pl.ds(0, tb * top_k)], xs_hbm.at[pl.ds(0, tb * top_k)], sem).wait()


def moe_dispatch(x, dest, pad_start, pad_cnt, n_rows):
    n, d = x.shape
    n_exp = pad_start.shape[0]
    tb = min(DISPATCH_TB, n)
    return pl.pallas_call(
        functools.partial(_dispatch_kernel, top_k=TOP_K, n_exp=n_exp, tb=tb),
        grid_spec=pltpu.PrefetchScalarGridSpec(
            num_scalar_prefetch=3,
            grid=(n // tb,),
            in_specs=[pl.BlockSpec((tb, d), lambda i, de, ps, pc: (i, 0))],
            out_specs=pl.BlockSpec(memory_space=pl.ANY),
            scratch_shapes=[pltpu.VMEM((8, d), F32),
                            pltpu.SemaphoreType.DMA(()),
                            pltpu.SemaphoreType.DMA(())],
        ),
        out_shape=jax.ShapeDtypeStruct((n_rows, d), F32),
        compiler_params=_cparams(("arbitrary",)),
        name="moe_dispatch",
    )(dest, pad_start, pad_cnt, x)


def _expert_kernel(blk_e_ref, blk_rows_ref, blk_src_ref,
                   x_ref, wg_ref, wu_ref, bg_ref, bu_ref, wd_ref, bd_ref, o_ref,
                   xb_s, h_s, wg_s, wu_s, wd_s, *, n_f):
    s = pl.program_id(0)
    t = pl.program_id(1)
    rows = blk_rows_ref[s]
    n_sub = MOE_SUPER // MOE_SUB

    def for_sub_blocks(fn):
        for r in range(n_sub):
            @pl.when(r * MOE_SUB < rows)
            def _():
                fn(pl.ds(r * MOE_SUB, MOE_SUB))

    @pl.when((rows > 0) & (t == 0))
    def _():
        def cast_x(sl):
            xb_s[sl, :] = x_ref[sl, :].astype(BF16)
        for_sub_blocks(cast_x)

    @pl.when((rows > 0) & (t < n_f))
    def _():
        wg_s[...] = wg_ref[0, 0].astype(BF16)
        wu_s[...] = wu_ref[0, 0].astype(BF16)

        def up_proj(sl):
            xb = xb_s[sl, :]
            hg = jnp.dot(xb, wg_s[...], preferred_element_type=F32) + bg_ref[0, 0]
            hu = jnp.dot(xb, wu_s[...], preferred_element_type=F32) + bu_ref[0, 0]
            gl = jnp.minimum(hg, SWIGLU_LIMIT)
            up = jnp.clip(hu, -SWIGLU_LIMIT, SWIGLU_LIMIT)
            h_s[t, sl, :] = (gl * jax.nn.sigmoid(SWIGLU_ALPHA * gl) * (up + 1.0)).astype(BF16)
        for_sub_blocks(up_proj)

    @pl.when((rows > 0) & (t >= n_f))
    def _():
        wd_s[...] = wd_ref[0, 0].astype(BF16)

        def down_proj(sl):
            h = jnp.concatenate([h_s[j, sl, :] for j in range(n_f)], axis=1)
            o_ref[sl, :] = jnp.dot(h, wd_s[...], preferred_element_type=F32) + bd_ref[0, 0]
        for_sub_blocks(down_proj)

        for r in range(1, n_sub):
            @pl.when(r * MOE_SUB >= rows)
            def _():
                o_ref[pl.ds(r * MOE_SUB, MOE_SUB), :] = jnp.zeros((MOE_SUB, o_ref.shape[1]), F32)


def moe_experts(xs, blk_e, blk_rows, blk_src, w_gu, b_gu, w_dn, b_dn, layer):
    n_rows, d = xs.shape
    n_layers, n_exp, _, two_f = w_gu.shape
    d_ff = two_f // 2
    n_super = n_rows // MOE_SUPER
    n_f = d_ff // MOE_TF
    n_d = d // MOE_TN

    def j1(s, t, rows):
        return jnp.where(rows[s] > 0, jnp.minimum(t, n_f - 1), n_f - 1)

    def j2(s, t, rows):
        return jnp.where(rows[s] > 0, jnp.maximum(t - n_f, 0), n_d - 1)

    return pl.pallas_call(
        functools.partial(_expert_kernel, n_f=n_f),
        grid_spec=pltpu.PrefetchScalarGridSpec(
            num_scalar_prefetch=3,
            grid=(n_super, n_f + n_d),
            in_specs=[
                pl.BlockSpec((MOE_SUPER, d), lambda s, t, be, br, bs: (bs[s], 0)),
                pl.BlockSpec((1, 1, d, MOE_TF),
                             lambda s, t, be, br, bs: (layer, be[s], 0, j1(s, t, br))),
                pl.BlockSpec((1, 1, d, MOE_TF),
                             lambda s, t, be, br, bs: (layer, be[s], 0, n_f + j1(s, t, br))),
                pl.BlockSpec((1, 1, 1, MOE_TF),
                             lambda s, t, be, br, bs: (layer, be[s], 0, j1(s, t, br))),
                pl.BlockSpec((1, 1, 1, MOE_TF),
                             lambda s, t, be, br, bs: (layer, be[s], 0, n_f + j1(s, t, br))),
                pl.BlockSpec((1, 1, d_ff, MOE_TN),
                             lambda s, t, be, br, bs: (layer, be[s], 0, j2(s, t, br))),
                pl.BlockSpec((1, 1, 1, MOE_TN),
                             lambda s, t, be, br, bs: (layer, be[s], 0, j2(s, t, br))),
            ],
            out_specs=pl.BlockSpec((MOE_SUPER, MOE_TN), lambda s, t, be, br, bs: (bs[s], j2(s, t, br))),
            scratch_shapes=[pltpu.VMEM((MOE_SUPER, d), BF16),
                            pltpu.VMEM((n_f, MOE_SUPER, MOE_TF), BF16),
                            pltpu.VMEM((d, MOE_TF), BF16),
                            pltpu.VMEM((d, MOE_TF), BF16),
                            pltpu.VMEM((d_ff, MOE_TN), BF16)],
        ),
        out_shape=jax.ShapeDtypeStruct((n_rows, d), F32),
        compiler_params=_cparams(("arbitrary", "arbitrary")),
        name="moe_experts",
    )(blk_e, blk_rows, blk_src, xs, w_gu, w_gu, b_gu.reshape(n_layers, n_exp, 1, two_f),
      b_gu.reshape(n_layers, n_exp, 1, two_f), w_dn, b_dn.reshape(n_layers, n_exp, 1, d))


def _combine_kernel(dest_ref, x_ref, tg_ref, g_ref, b_ref, yb_hbm, o_ref, buf, sem,
                    *, top_k, alpha, tb):
    i = pl.program_id(0)
    n_blk = pl.num_programs(0)

    def start_rows(blk, slot):
        base = blk * (tb * top_k)

        def body(g, c):
            for u in range(DMA_UNROLL):
                r = g * DMA_UNROLL + u
                for k in range(top_k):
                    pltpu.make_async_copy(
                        yb_hbm.at[pl.ds(dest_ref[base + r * top_k + k], 1)],
                        buf.at[slot, pl.ds(k * tb + r, 1)], sem.at[slot]).start()
            return c
        lax.fori_loop(0, tb // DMA_UNROLL, body, 0)

    @pl.when(i == 0)
    def _():
        start_rows(0, 0)

    @pl.when(i + 1 < n_blk)
    def _():
        start_rows(i + 1, (i + 1) % 2)

    slot = i % 2
    pltpu.make_async_copy(yb_hbm.at[pl.ds(0, tb * top_k)], buf.at[slot], sem.at[slot]).wait()

    tg = tg_ref[...]
    acc = alpha * x_ref[...]
    for k in range(top_k):
        acc = acc + tg[:, k:k + 1] * buf[slot, k * tb:(k + 1) * tb]
    o_ref[...] = _ln_rows(acc, g_ref[...], b_ref[...])


def moe_combine_ln(x, yb, dest, tg, g, b, alpha):
    n, d = x.shape
    tb = COMBINE_TB
    return pl.pallas_call(
        functools.partial(_combine_kernel, top_k=TOP_K, alpha=alpha, tb=tb),
        grid_spec=pltpu.PrefetchScalarGridSpec(
            num_scalar_prefetch=1,
            grid=(n // tb,),
            in_specs=[pl.BlockSpec((tb, d), lambda i, dr: (i, 0)),
                      pl.BlockSpec((tb, LANES), lambda i, dr: (i, 0)),
                      pl.BlockSpec((1, d), lambda i, dr: (0, 0)),
                      pl.BlockSpec((1, d), lambda i, dr: (0, 0)),
                      pl.BlockSpec(memory_space=pl.ANY)],
            out_specs=pl.BlockSpec((tb, d), lambda i, dr: (i, 0)),
            scratch_shapes=[pltpu.VMEM((2, TOP_K * tb, d), F32),
                            pltpu.SemaphoreType.DMA((2,))],
        ),
        out_shape=jax.ShapeDtypeStruct((n, d), F32),
        compiler_params=_cparams(("arbitrary",)),
        name="moe_combine_ln",
    )(dest, x, tg, g.reshape(1, d), b.reshape(1, d), yb)


def moe_layer(x, w_router, b_router, w_gu, b_gu, w_dn, b_dn, layer, ln_g, ln_b, alpha):
    n, d = x.shape
    n_exp = w_router.shape[1]
    m = n * TOP_K
    n_super = -(-m // MOE_SUPER) + n_exp
    n_rows = n_super * MOE_SUPER

    ti, tg, cnt = moe_router(x, w_router, b_router)
    counts = cnt[0].astype(jnp.int32)
    nsup = (counts + MOE_SUPER - 1) // MOE_SUPER
    sup_end = jnp.cumsum(nsup)
    sup_start = sup_end - nsup
    row_start = sup_start * MOE_SUPER
    top_i = ti[:, :TOP_K]
    rank = ti[:, TOP_K:2 * TOP_K]
    dest = (row_start[top_i] + rank).reshape(m)
    s_idx = jnp.arange(n_super, dtype=jnp.int32)
    n_used = sup_end[-1]
    used = s_idx < n_used
    src = jnp.where(used, s_idx, n_used - 1)
    blk_e = jnp.minimum(jnp.searchsorted(sup_end, src, side="right"), n_exp - 1).astype(jnp.int32)
    blk_rows = jnp.clip(counts[blk_e] - (src - sup_start[blk_e]) * MOE_SUPER, 0, MOE_SUPER)
    blk_rows = jnp.where(used, blk_rows, 0).astype(jnp.int32)
    pad_start = row_start + counts
    pad_cnt = (-counts) % MOE_SUB

    xs = moe_dispatch(x, dest, pad_start, pad_cnt, n_rows)
    yb = moe_experts(xs, blk_e, blk_rows, src.astype(jnp.int32), w_gu, b_gu, w_dn, b_dn, layer)
    return moe_combine_ln(x, yb, dest, tg, ln_g, ln_b, alpha)


def _head_sum_matrix(n_b, scale):
    row = lax.broadcasted_iota(jnp.int32, (LANES, LANES), 0)
    col = lax.broadcasted_iota(jnp.int32, (LANES, LANES), 1)
    return jnp.where(row // n_b == col // n_b, scale, 0.0).astype(BF16)


def _split_dot(dot_fn, x, terms=3):
    acc = None
    for _ in range(terms):
        piece = x.astype(BF16)
        part = dot_fn(piece)
        acc = part if acc is None else acc + part
        x = x - piece.astype(F32)
    return acc


def _per_head(x, mat):
    outs = [_split_dot(lambda a: jnp.dot(a, mat, preferred_element_type=F32),
                       x[:, s * LANES:(s + 1) * LANES]) for s in range(x.shape[1] // LANES)]
    return jnp.concatenate(outs, axis=1)


def _rwkv_prep_kernel(p_ref, prev_ref, mu_ref, w0_ref, w2_ref, a0_ref, a2_ref, g2_ref, kk_ref,
                      ka_ref, r_ref, k_ref, v_ref, lw_ref, al_ref, be_ref, g_ref,
                      *, d_b, wl, al, n_b):
    p = p_ref[...]
    xm = p + (prev_ref[...] - p) * mu_ref[...]
    r = xm[:, :d_b]
    k = xm[:, d_b:2 * d_b]
    v = xm[:, 2 * d_b:3 * d_b]
    wd = xm[:, 3 * d_b:3 * d_b + wl]
    ad = xm[:, 3 * d_b + wl:3 * d_b + wl + al]
    gd = xm[:, 3 * d_b + wl + al:]
    dotb = lambda a, w_ref: jnp.dot(a.astype(BF16), w_ref[...].astype(BF16),
                                    preferred_element_type=F32)
    wlog = -jax.nn.softplus(-(w0_ref[...] + dotb(jnp.tanh(wd), w2_ref))) - 0.5
    a = jax.nn.sigmoid(a0_ref[...] + dotb(ad, a2_ref))
    kk = k * kk_ref[...]
    norm = jnp.sqrt(_per_head(kk * kk, _head_sum_matrix(n_b, 1.0)))
    kk = kk / jnp.maximum(norm, 1e-12)
    r_ref[...] = r
    k_ref[...] = k * (1.0 + (a - 1.0) * ka_ref[...])
    v_ref[...] = v
    lw_ref[...] = -jnp.exp(wlog)
    al_ref[...] = -kk
    be_ref[...] = kk * a
    g_ref[...] = dotb(jax.nn.sigmoid(gd), g2_ref)


def rwkv_prep(p, row0, prev, mu, w0, w2, a0, a2, g2, k_k, k_a, n_b, tm=256):
    n, pb = prev.shape
    blk0 = row0 // tm
    d_b = w0.shape[0]
    wl, al = w2.shape[0], a2.shape[0]
    row = lambda c: pl.BlockSpec((tm, c), lambda i: (i, 0))
    full = lambda a: pl.BlockSpec(a.shape, lambda i: (0, 0))
    vec = lambda a: a.reshape(1, -1)
    consts = [vec(mu), vec(w0), w2, vec(a0), a2, g2, vec(k_k), vec(k_a)]
    return pl.pallas_call(
        functools.partial(_rwkv_prep_kernel, d_b=d_b, wl=wl, al=al, n_b=n_b),
        grid=(n // tm,),
        in_specs=[pl.BlockSpec((tm, pb), lambda i: (blk0 + i, 0)), row(pb)] + [full(c) for c in consts],
        out_specs=[row(d_b)] * 7,
        out_shape=[jax.ShapeDtypeStruct((n, d_b), F32)] * 7,
        compiler_params=_cparams(("arbitrary",)),
        name="rwkv_prep",
    )(p, prev, *consts)


def _rwkv_chunk_kernel(r_ref, k_ref, v_ref, lw_ref, al_ref, be_ref, g_ref, s0_ref, gng_ref,
                       gnb_ref, rk_ref, o_ref, s_out_ref, st_ref, *, n_b, chunk, seg, n_pairs):
    c = pl.program_id(1)
    c2 = 2 * chunk
    n_seg = chunk // seg

    @pl.when(c == 0)
    def _():
        st_ref[...] = s0_ref[...]

    left = lambda mat: (lambda a: jnp.dot(mat, a, preferred_element_type=F32))
    right = lambda mat: (lambda a: jnp.dot(a, mat, preferred_element_type=F32))
    dot = lambda a, b: jnp.dot(a.astype(BF16), b.astype(BF16), preferred_element_type=F32)
    dot_nt = lambda a, b: lax.dot_general(a.astype(BF16), b.astype(BF16), (((1,), (1,)), ((), ())),
                                          preferred_element_type=F32)
    dot_tn = lambda a, b: lax.dot_general(a.astype(BF16), b.astype(BF16), (((0,), (0,)), ((), ())),
                                          preferred_element_type=F32)
    lane = lax.broadcasted_iota(jnp.int32, (chunk, LANES), 1)
    head_a = lane < n_b
    stack = lambda x: jnp.concatenate([jnp.where(head_a, x, 0.0), jnp.where(head_a, 0.0, x)], axis=0)
    twice = lambda x: jnp.concatenate([x, x], axis=0)
    row2 = lax.broadcasted_iota(jnp.int32, (c2, c2), 0)
    col2 = lax.broadcasted_iota(jnp.int32, (c2, c2), 1)
    same = (row2 // seg) == (col2 // seg)
    strict = same & (row2 > col2)
    incl = same & (row2 >= col2)
    rowc = lax.broadcasted_iota(jnp.int32, (chunk, chunk), 0)
    colc = lax.broadcasted_iota(jnp.int32, (chunk, chunk), 1)
    cum = jnp.where((rowc >= colc) & (rowc // seg == colc // seg), 1.0, 0.0).astype(BF16)
    rl = lax.broadcasted_iota(jnp.int32, (LANES, LANES), 0)
    cl_ = lax.broadcasted_iota(jnp.int32, (LANES, LANES), 1)
    blockdiag = (rl // n_b) == (cl_ // n_b)
    sum_mat = _head_sum_matrix(n_b, 1.0)
    segs = [slice(q * seg, (q + 1) * seg) for q in range(n_seg)]
    cat0 = lambda parts: parts[0] if len(parts) == 1 else jnp.concatenate(parts, axis=0)

    for p in range(n_pairs):
        sl = slice(p * LANES, (p + 1) * LANES)
        r, k, v = r_ref[:, sl], k_ref[:, sl], v_ref[:, sl]
        lw, al, be = lw_ref[:, sl], al_ref[:, sl], be_ref[:, sl]
        cl = _split_dot(left(cum), lw)
        e_neg = jnp.exp(-cl)
        at = al * jnp.exp(cl - lw)
        rt = r * jnp.exp(cl)
        bt = be * e_neg
        kt = k * e_neg
        gram = dot_nt(jnp.concatenate([stack(at), stack(rt)], axis=0),
                      jnp.concatenate([twice(bt), twice(kt)], axis=0))
        nm = jnp.where(strict, gram[:c2, :c2], 0.0)
        aak = jnp.where(strict, gram[:c2, c2:], 0.0)
        arb = jnp.where(incl, gram[c2:, :c2], 0.0)
        ark = jnp.where(incl, gram[c2:, c2:], 0.0)
        ms = [st_ref[q, p] for q in range(n_seg)]
        fs = [dot(jnp.concatenate([at[sq], rt[sq]], axis=0), ms[q]) for q, sq in enumerate(segs)]
        a_s0 = cat0([f[:seg] for f in fs])
        r_s0 = cat0([f[seg:] for f in fs])
        vs = stack(v)
        u = stack(a_s0) + dot(aak, vs)
        pw = nm
        n = 1
        while n < seg:
            u = u + dot(pw, u)
            n *= 2
            if n < seg:
                pw = dot(pw, pw)
        ys = stack(r_s0) + dot(jnp.concatenate([arb, ark], axis=1), jnp.concatenate([u, vs], axis=0))
        y = ys[:chunk] + ys[chunk:]
        up = u[:chunk] + u[chunk:]
        ends = [cl[sq.stop - 1:sq.stop, :] for sq in segs]
        to_end = jnp.exp(cat0([jnp.broadcast_to(e, (seg, LANES)) for e in ends]) - cl)
        b_end, k_end = be * to_end, k * to_end
        for q, sq in enumerate(segs):
            upd = dot_tn(jnp.concatenate([b_end[sq], k_end[sq]], axis=0),
                         jnp.concatenate([up[sq], v[sq]], axis=0))
            pc_col = jnp.transpose(jnp.broadcast_to(jnp.exp(ends[q]), (8, LANES)))[:, 0:1]
            st_ref[q, p] = pc_col * ms[q] + jnp.where(blockdiag, upd, 0.0)

        stats = _split_dot(right(sum_mat), jnp.concatenate([y, y * y, r * k * rk_ref[:, sl]], axis=0))
        mean = stats[:chunk] * (1.0 / n_b)
        var = stats[chunk:c2] * (1.0 / n_b) - mean * mean
        yn = (y - mean) * lax.rsqrt(var + RWKV_GN_EPS) * gng_ref[:, sl] + gnb_ref[:, sl]
        o_ref[:, sl] = (yn + stats[c2:] * v) * g_ref[:, sl]

    @pl.when(c == pl.num_programs(1) - 1)
    def _():
        s_out_ref[...] = st_ref[...]


def rwkv_chunked(r, k, v, lw, al, be, g, s0_bd, gn_g, gn_b, r_k, bsz, t, n_b, chunk,
                 row0=0, n_total=None, out_buf=None):
    n, d_b = r.shape
    n_pairs = d_b // LANES
    seg = min(t, chunk)
    n_seg = chunk // seg
    n_c = t // seg
    row = pl.BlockSpec((chunk, d_b), lambda b, c: (b * n_c + c, 0))
    vec = pl.BlockSpec((1, d_b), lambda b, c: (0, 0))
    st = pl.BlockSpec((n_seg, n_pairs, LANES, LANES), lambda b, c: (b, 0, 0, 0))
    blk0 = row0 // chunk
    out_row = pl.BlockSpec((chunk, d_b), lambda b, c: (blk0 + b * n_c + c, 0))
    return _shared_rows_call(
        functools.partial(_rwkv_chunk_kernel, n_b=n_b, chunk=chunk, seg=seg, n_pairs=n_pairs),
        11, out_buf,
        grid=(bsz // n_seg, n_c),
        in_specs=[row] * 7 + [st, vec, vec, vec],
        out_specs=[out_row, st],
        out_shape=[jax.ShapeDtypeStruct((n if n_total is None else n_total, d_b), F32),
                   jax.ShapeDtypeStruct(s0_bd.shape, F32)],
        scratch_shapes=[pltpu.VMEM((n_seg, n_pairs, LANES, LANES), F32)],
        compiler_params=_cparams(("arbitrary", "arbitrary")),
        name="rwkv_chunked",
    )(r, k, v, lw, al, be, g, s0_bd, gn_g.reshape(1, d_b), gn_b.reshape(1, d_b),
      r_k.reshape(1, d_b))


def _state_to_blockdiag(s):
    bsz, n_h, n_v, n_k = s.shape
    st = jnp.swapaxes(s, 2, 3).reshape(bsz, n_h // 2, 2, n_k, n_v)
    z = jnp.zeros_like(st[:, :, 0])
    top = jnp.concatenate([st[:, :, 0], z], axis=-1)
    bot = jnp.concatenate([z, st[:, :, 1]], axis=-1)
    return jnp.concatenate([top, bot], axis=-2)


def _blockdiag_to_state(m, n_b):
    bsz, n_pairs = m.shape[:2]
    a = m[:, :, :n_b, :n_b]
    b = m[:, :, n_b:, n_b:]
    st = jnp.stack([a, b], axis=2).reshape(bsz, n_pairs * 2, n_b, n_b)
    return jnp.swapaxes(st, 2, 3)


def rwkv7_group(p_all, row0, bsz, t, shift_prev, s0, mu, w0, w2, a0, a2, g2, k_k, k_a, r_k, gn_g,
                gn_b, chunk, out_buf=None):
    pb = p_all.shape[1]
    n_b = s0.shape[2]
    p = p_all[row0:row0 + bsz * t].reshape(bsz, t, pb)
    prev = jnp.concatenate([shift_prev[:, None], p[:, :-1]], axis=1).reshape(bsz * t, pb)
    tm = 256
    r, k, v, lw, al, be, g = rwkv_prep(p_all, row0, prev, mu, w0, w2, a0, a2, g2, k_k, k_a, n_b, tm=tm)
    o, m = rwkv_chunked(r, k, v, lw, al, be, g, _state_to_blockdiag(s0), gn_g, gn_b, r_k,
                        bsz, t, n_b, chunk, row0=row0, n_total=p_all.shape[0], out_buf=out_buf)
    return o, _blockdiag_to_state(m, n_b), p[:, -1]


def _hgrn_kernel(q_ref, f_ref, i_ref, gate_ref, lb_ref, ng_ref, s0_ref, o_ref, s_out_ref, st_ref,
                 *, n_h, sub, tb):
    c = pl.program_id(1)

    @pl.when(c == 0)
    def _():
        st_ref[...] = s0_ref[0]

    row = lax.broadcasted_iota(jnp.int32, (sub, sub), 0)
    col = lax.broadcasted_iota(jnp.int32, (sub, sub), 1)
    cum = jnp.where(row >= col, 1.0, 0.0).astype(BF16)
    t_idx = lax.broadcasted_iota(jnp.int32, (sub, LANES), 0)
    bf = lambda x: x.astype(BF16)

    def body(j, h0):
        rows = pl.ds(pl.multiple_of(j * sub, sub), sub)
        for h in range(h0, h0 + HGRN_HEAD_GROUP):
            sl = slice(h * LANES, (h + 1) * LANES)
            lb = lb_ref[:, sl]
            s_mat = st_ref[h]
            q = jax.nn.silu(q_ref[rows, sl])
            z = f_ref[rows, sl]
            logf = jnp.log(lb + (1.0 - lb) * jax.nn.sigmoid(z))
            kk = (1.0 - lb) * jax.nn.sigmoid(-z)
            v = i_ref[rows, sl]
            b = _split_dot(lambda a: jnp.dot(cum, a, preferred_element_type=F32), logf)
            o = jnp.dot(bf(q * jnp.exp(b)), bf(s_mat), preferred_element_type=F32)
            vb = bf(v).astype(F32)
            for s in range(sub):
                dec = jnp.exp(jnp.where(t_idx >= s, b - b[s:s + 1, :], -jnp.inf))
                att = jnp.sum(q * kk[s:s + 1, :] * dec, axis=-1, keepdims=True)
                o = o + bf(att).astype(F32) * vb[s:s + 1, :]
            bl = b[sub - 1:sub, :]
            kd = bf(kk * jnp.exp(bl - b))
            upd = lax.dot_general(kd, bf(v), (((0,), (0,)), ((), ())), preferred_element_type=F32)
            decay = jnp.exp(bl)
            st_ref[h] = jnp.transpose(jnp.broadcast_to(decay, (8, LANES)))[:, 0:1] * s_mat + upd
            o = o * lax.rsqrt(jnp.mean(o * o, axis=-1, keepdims=True) + LN_EPS) * ng_ref[...]
            o_ref[rows, sl] = o * jax.nn.silu(gate_ref[rows, sl])

    for h0 in range(0, n_h, HGRN_HEAD_GROUP):
        def group_body(j, carry, h0=h0):
            body(j, h0)
            return carry
        lax.fori_loop(0, tb // sub, group_body, 0)

    @pl.when(c == pl.num_programs(1) - 1)
    def _():
        s_out_ref[0] = st_ref[...]


def hgrn2(proj, row0, s0, lb, norm_g, bsz, t, out_buf=None):
    n = proj.shape[0]
    n_h, dk, dv = s0.shape[1:]
    d_a = n_h * dk
    sub = min(CHUNK_A, t)
    tb = min(256, t)
    n_c = t // tb
    blk0 = row0 // tb
    col = lambda which: pl.BlockSpec((tb, d_a), lambda b, c: (blk0 + b * n_c + c, which))
    st = pl.BlockSpec((1, n_h, dk, dv), lambda b, c: (b, 0, 0, 0))
    return _shared_rows_call(
        functools.partial(_hgrn_kernel, n_h=n_h, sub=sub, tb=tb), 7, out_buf,
        grid=(bsz, n_c),
        in_specs=[col(0), col(1), col(2), col(3),
                  pl.BlockSpec((1, d_a), lambda b, c: (0, 0)),
                  pl.BlockSpec((1, dv), lambda b, c: (0, 0)), st],
        out_specs=[col(0), st],
        out_shape=[jax.ShapeDtypeStruct((n, d_a), F32), jax.ShapeDtypeStruct(s0.shape, F32)],
        scratch_shapes=[pltpu.VMEM((n_h, dk, dv), F32)],
        compiler_params=_cparams(("arbitrary", "arbitrary")),
        name="hgrn2",
    )(proj, proj, proj, proj, lb.reshape(1, d_a), norm_g.reshape(1, dv), s0)


def _pool_ln_kernel(x_ref, halo_ref, w_ref, sc_ref, g_ref, b_ref, o_ref, xx_ref,
                    *, windows, start_pos, alpha, tb, halo):
    c = pl.program_id(1)
    d = x_ref.shape[1]
    pc = d // len(windows)
    xx_ref[0:halo, :] = halo_ref[0, 0]
    xx_ref[halo:halo + tb, :] = x_ref[...]
    x = x_ref[...]
    pos = start_pos + c * tb + lax.broadcasted_iota(jnp.int32, (tb, 1), 0)
    ys = []
    for gi, w in enumerate(windows):
        sl = slice(gi * pc, (gi + 1) * pc)
        acc = x[:, sl]
        for back in range(1, w):
            acc = acc + xx_ref[halo - back:halo - back + tb, sl]
        cnt = jnp.minimum(pos + 1, w).astype(F32)
        u = acc / cnt - x[:, sl]
        ys.append(jnp.dot(u.astype(BF16), w_ref[gi], preferred_element_type=F32))
    y = jnp.concatenate(ys, axis=1) * sc_ref[...]
    o_ref[...] = _ln_rows(alpha * x + y, g_ref[...], b_ref[...])


def pool_residual_ln(x, row0, hist, start_pos, w_pool, scale, g, b, alpha, bsz, t, out_buf=None):
    n, d = x.shape
    halo = 16
    tb = min(256, t)
    n_c = t // tb
    blk0 = row0 // tb
    x4 = x[row0:row0 + bsz * t].reshape(bsz, n_c, tb, d)
    first = jnp.concatenate([jnp.zeros((bsz, halo - hist.shape[1], d), F32), hist], axis=1)
    halos = first[:, None]
    if n_c > 1:
        halos = jnp.concatenate([halos, x4[:, :-1, tb - halo:]], axis=1)
    vec = lambda a: pl.BlockSpec((1, d), lambda bi, c: (0, 0))
    rows = pl.BlockSpec((tb, d), lambda bi, c: (blk0 + bi * n_c + c, 0))
    return _shared_rows_call(
        functools.partial(_pool_ln_kernel, windows=POOL_WINDOWS, start_pos=start_pos,
                          alpha=alpha, tb=tb, halo=halo), 6, out_buf,
        grid=(bsz, n_c),
        in_specs=[rows,
                  pl.BlockSpec((1, 1, halo, d), lambda bi, c: (bi, c, 0, 0)),
                  pl.BlockSpec(w_pool.shape, lambda bi, c: (0, 0, 0)),
                  vec(scale), vec(g), vec(b)],
        out_specs=rows,
        out_shape=jax.ShapeDtypeStruct((n, d), F32),
        scratch_shapes=[pltpu.VMEM((halo + tb, d), F32)],
        compiler_params=_cparams(("arbitrary", "arbitrary")),
        name="pool_residual_ln",
    )(x, halos, w_pool.astype(BF16), scale.reshape(1, d), g.reshape(1, d), b.reshape(1, d))


def _hgrn2_group(q, f, i, gate, s0, lb, norm_g):
    bsz, t, d_a = q.shape
    n_h, dk, dv = s0.shape[1:]
    q = jax.nn.silu(q)
    logf = jnp.log(lb + (1.0 - lb) * jax.nn.sigmoid(f))
    k = (1.0 - lb) * jax.nn.sigmoid(-f)
    tp = -(-t // CHUNK_A) * CHUNK_A
    pad = [(0, 0), (0, tp - t), (0, 0)]
    q, k, v, logf = [jnp.pad(a, pad) for a in (q, k, i, logf)]
    nc = tp // CHUNK_A
    blocks = lambda a, dd: a.reshape(bsz, nc, CHUNK_A, n_h, dd).transpose(1, 0, 3, 2, 4)
    mask = jnp.tril(jnp.ones((CHUNK_A, CHUNK_A), bool))

    def step(S, inp):
        qc, kc, vc, gc = inp
        b = jnp.cumsum(gc, axis=2)
        diff = b[:, :, :, None, :] - b[:, :, None, :, :]
        dec = jnp.exp(jnp.where(mask[:, :, None], diff, -jnp.inf))
        att = jnp.sum(qc[:, :, :, None, :] * kc[:, :, None, :, :] * dec, axis=-1)
        o = (jnp.einsum('bhts,bhsv->bhtv', att, vc)
             + jnp.einsum('bhtd,bhdv->bhtv', qc * jnp.exp(b), S))
        bl = b[:, :, -1:, :]
        S = (jnp.exp(bl[:, :, 0, :])[..., None] * S
             + jnp.einsum('bhsd,bhsv->bhdv', kc * jnp.exp(bl - b), vc))
        return S, o

    s_t, o = lax.scan(step, s0, (blocks(q, dk), blocks(k, dk), blocks(v, dv), blocks(logf, dk)))
    o = o.transpose(1, 0, 3, 2, 4).reshape(bsz, tp, n_h, dv)[:, :t]
    o = o * lax.rsqrt(jnp.mean(o * o, -1, keepdims=True) + LN_EPS) * norm_g
    return o.reshape(bsz, t, d_a) * jax.nn.silu(gate), s_t


def _rwkv7_group(p, shift_prev, s0, mu, w0, w2, a0, a2, g2, k_k, k_a, r_k, gn_g, gn_b, chunk):
    bsz, t, _ = p.shape
    d_b = w0.shape[0]
    n_h, n_b = s0.shape[1], s0.shape[2]
    wl, al_ = w2.shape[0], a2.shape[0]
    prev = jnp.concatenate([shift_prev[:, None], p[:, :-1]], axis=1)
    xm = p + (prev - p) * mu
    r, k, v, wd, ad, gd = jnp.split(
        xm, [d_b, 2 * d_b, 3 * d_b, 3 * d_b + wl, 3 * d_b + wl + al_], axis=-1)
    wlog = -jax.nn.softplus(-(w0 + jnp.tanh(wd) @ w2)) - 0.5
    logw = -jnp.exp(wlog)
    a = jax.nn.sigmoid(a0 + ad @ a2)
    g = jax.nn.sigmoid(gd) @ g2
    hs = lambda x: x.reshape(bsz, t, n_h, n_b)
    r, k, v, logw, a = map(hs, (r, k, v, logw, a))
    kk = k * k_k.reshape(n_h, n_b)
    kk = kk / jnp.maximum(jnp.sqrt(jnp.sum(kk * kk, -1, keepdims=True)), 1e-12)
    k2 = k * (1.0 + (a - 1.0) * k_a.reshape(n_h, n_b))
    al = -kk
    be = kk * a
    c = chunk
    nc = t // c
    ch = lambda x: x.reshape(bsz, nc, c, n_h, n_b).transpose(1, 0, 3, 2, 4)
    strict = jnp.tril(jnp.ones((c, c), bool), -1)
    incl = jnp.tril(jnp.ones((c, c), bool))
    ein = functools.partial(jnp.einsum, precision=HIGHEST)

    def step(S, inp):
        rc, kc, vc, lc, ac, bc = inp
        cl = jnp.cumsum(lc, axis=2)
        at = ac * jnp.exp(cl - lc)
        rt = rc * jnp.exp(cl)
        bt = bc * jnp.exp(-cl)
        kt = kc * jnp.exp(-cl)
        nm = jnp.where(strict, ein('bhck,bhik->bhci', at, bt), 0.0)
        aak = jnp.where(strict, ein('bhck,bhik->bhci', at, kt), 0.0)
        arb = jnp.where(incl, ein('bhck,bhik->bhci', rt, bt), 0.0)
        ark = jnp.where(incl, ein('bhck,bhik->bhci', rt, kt), 0.0)
        u = ein('bhck,bhvk->bhcv', at, S) + ein('bhci,bhiv->bhcv', aak, vc)
        pw = nm
        n = 1
        while n < c:
            u = u + ein('bhci,bhiv->bhcv', pw, u)
            n *= 2
            if n < c:
                pw = ein('bhci,bhij->bhcj', pw, pw)
        y = (ein('bhck,bhvk->bhcv', rt, S) + ein('bhci,bhiv->bhcv', arb, u)
             + ein('bhci,bhiv->bhcv', ark, vc))
        sn = jnp.exp(cl[:, :, -1])[:, :, None, :] * (
            S + ein('bhiv,bhik->bhvk', u, bt) + ein('bhiv,bhik->bhvk', vc, kt))
        return sn, y

    s_t, y = lax.scan(step, s0, tuple(map(ch, (r, k2, v, logw, al, be))))
    y = y.transpose(1, 0, 3, 2, 4).reshape(bsz, t, n_h, n_b)
    mean = jnp.mean(y, -1, keepdims=True)
    var = jnp.mean(jnp.square(y - mean), -1, keepdims=True)
    y = (y - mean) * lax.rsqrt(var + RWKV_GN_EPS) * gn_g.reshape(n_h, n_b) + gn_b.reshape(n_h, n_b)
    y = y + jnp.sum(r * k2 * r_k, -1, keepdims=True) * v
    return y.reshape(bsz, t, d_b) * g, s_t, p[:, -1]


def _pool_group(x, hist, start_pos, w_pool, scale):
    bsz, t, d = x.shape
    n_g = len(POOL_WINDOWS)
    pc = d // n_g
    hlen = hist.shape[1]
    xx = jnp.concatenate([hist, x], axis=1)
    pos = jnp.arange(t) + start_pos
    outs = []
    for gi, w in enumerate(POOL_WINDOWS):
        sl = slice(gi * pc, (gi + 1) * pc)
        acc = x[..., sl]
        for back in range(1, w):
            acc = acc + xx[:, hlen - back:hlen - back + t, sl]
        cnt = jnp.minimum(pos + 1, w).astype(F32)[None, :, None]
        outs.append(acc / cnt - x[..., sl])
    u = jnp.stack(outs, axis=2)
    y = jnp.einsum('btgc,gcd->btgd', u, w_pool).reshape(bsz, t, d) * scale
    return y, xx[:, -hlen:]


def kernel(x_prompt, x_sample, state_hgrn, state_rwkv, state_rwkv_shift, state_pool, mix_w_in, hgrn_lb, hgrn_norm_g, rwkv_mu, rwkv_w0, rwkv_w2, rwkv_a0, rwkv_a2, rwkv_g2, rwkv_k_k, rwkv_k_a, rwkv_r_k, rwkv_gn_g, rwkv_gn_b, mix_w_out, pool_w, pool_scale, ln1_g, ln1_b, ln2_g, ln2_b, moe_w_router, moe_b_router, moe_w_gu, moe_b_gu, moe_w_dn, moe_b_dn):
    bp, tp, d = x_prompt.shape
    bs, ts, _ = x_sample.shape
    depth = ln1_g.shape[0]
    n_h_a, dk_a, dv_a = state_hgrn.shape[2:]
    n_h_b, n_b = state_rwkv.shape[2:4]
    d_a = n_h_a * dk_a
    pb = state_rwkv_shift.shape[2]
    hist = state_pool.shape[2]
    n_p = bp * tp
    alpha = (2 * depth) ** 0.25

    lb_p = jax.nn.softmax(hgrn_lb, axis=0)
    lb_all = jnp.cumsum(lb_p, axis=0) - lb_p[0]
    x = jnp.concatenate([x_prompt.reshape(n_p, d), x_sample.reshape(bs * ts, d)], axis=0)
    hg_p, hg_s, rw_p, rw_s, sh_p, sh_s, pl_p, pl_s = [], [], [], [], [], [], [], []
    for l in range(depth):
        if l % 2 == 0:
            e = l // 2
            p_a = matmul(x, mix_w_in, e, 0, 4 * d_a)
            p_b = matmul(x, mix_w_in, e, 4 * d_a, pb)
            rw = (rwkv_mu[e], rwkv_w0[e], rwkv_w2[e], rwkv_a0[e], rwkv_a2[e], rwkv_g2[e],
                  rwkv_k_k[e], rwkv_k_a[e], rwkv_r_k[e], rwkv_gn_g[e], rwkv_gn_b[e])
            o_a = o_b = None
            for grp, (lo, hi, bsz, t) in enumerate(((0, n_p, bp, tp), (n_p, n_p + bs * ts, bs, ts))):
                if grp == 0:
                    s_h = jnp.zeros((bsz, n_h_a, dk_a, dv_a), F32)
                    s_r = jnp.zeros((bsz, n_h_b, n_b, n_b), F32)
                    s_s = jnp.zeros((bsz, pb), F32)
                else:
                    s_h, s_r, s_s = state_hgrn[e], state_rwkv[e], state_rwkv_shift[e]
                o_a, n_h = hgrn2(p_a, lo, s_h, lb_all[e], hgrn_norm_g[e], bsz, t, out_buf=o_a)
                o_b, n_r, n_s = rwkv7_group(p_b, lo, bsz, t, s_s, s_r, *rw, chunk=RWKV_CHUNK,
                                            out_buf=o_b)
                (hg_p, hg_s)[grp].append(n_h)
                (rw_p, rw_s)[grp].append(n_r)
                (sh_p, sh_s)[grp].append(n_s)
            w_out = mix_w_out[e].astype(BF16)
            x = proj_residual_ln(x, [o_a, o_b], [w_out[:d_a], w_out[d_a:]], ln1_g[l], ln1_b[l], alpha)
        else:
            j = l // 2
            x_p, x_s = x[:n_p], x[n_p:]
            hist_p = jnp.zeros((bp, hist, d), F32)
            pl_p.append(jnp.concatenate([hist_p, x_p.reshape(bp, tp, d)], axis=1)[:, -hist:])
            pl_s.append(jnp.concatenate([state_pool[j], x_s.reshape(bs, ts, d)], axis=1)[:, -hist:])
            pool = (pool_w[j], pool_scale[j], ln1_g[l], ln1_b[l], alpha)
            x_new = pool_residual_ln(x, 0, hist_p, 0, *pool, bp, tp)
            x = pool_residual_ln(x, n_p, state_pool[j], PAST_LEN, *pool, bs, ts, out_buf=x_new)
        x = moe_layer(x, moe_w_router[l], moe_b_router[l], moe_w_gu, moe_b_gu, moe_w_dn, moe_b_dn, l,
                      ln2_g[l], ln2_b[l], alpha)
    return (x[:n_p].reshape(bp, tp, d), x[n_p:].reshape(bs, ts, d),
            jnp.stack(hg_p), jnp.stack(hg_s), jnp.stack(rw_p), jnp.stack(rw_s),
            jnp.stack(sh_p), jnp.stack(sh_s), jnp.stack(pl_p), jnp.stack(pl_s))
```

```python
import functools

import jax
import jax.numpy as jnp
from jax import lax
from jax.experimental import pallas as pl
from jax.experimental.pallas import tpu as pltpu

F32 = jnp.float32
BF16 = jnp.bfloat16

POOL_WINDOWS = (2, 4, 8, 16)
TOP_K = 4
SWIGLU_LIMIT = 7.0
SWIGLU_ALPHA = 1.702
LN_EPS = 1e-5
RWKV_GN_EPS = 64e-5
PAST_LEN = 16384

VMEM_LIMIT_BYTES = 56 * 1024 * 1024
LANES = 128

MOE_SUPER = 1280
MOE_SUB = 640
MOE_TF = 256
MOE_TN = 256
ROUTER_TM = 512
COMBINE_TB = 128
HGRN_SUB = 16
HGRN_HEAD_GROUP = 8
RWKV_SEQS_PER_STEP = 1
RWKV_CHUNK = 64
DISPATCH_TB = 256
DMA_UNROLL = 8


def _cparams(sem):
    return pltpu.CompilerParams(dimension_semantics=sem, vmem_limit_bytes=VMEM_LIMIT_BYTES)


def _shared_rows_call(kernel, n_in, out_buf, **kwargs):
    if out_buf is None:
        return pl.pallas_call(kernel, **kwargs)
    body = lambda *refs: kernel(*refs[:n_in], *refs[n_in + 1:])
    kwargs["in_specs"] = list(kwargs["in_specs"]) + [pl.BlockSpec(memory_space=pl.ANY)]
    call = pl.pallas_call(body, input_output_aliases={n_in: 0}, **kwargs)
    return lambda *args: call(*args, out_buf)


def _mm_kernel(x_ref, w_ref, o_ref):
    o_ref[...] = jnp.dot(x_ref[...].astype(BF16), w_ref[0].astype(BF16),
                         preferred_element_type=F32)


def matmul(x, w, layer, col0, n, tm=512, tn=1024):
    m, k = x.shape
    tm = min(tm, m)
    assert col0 % tn == 0
    c0 = col0 // tn
    return pl.pallas_call(
        _mm_kernel,
        grid=(pl.cdiv(n, tn), pl.cdiv(m, tm)),
        in_specs=[pl.BlockSpec((tm, k), lambda j, i: (i, 0)),
                  pl.BlockSpec((1, k, tn), lambda j, i: (layer, 0, c0 + j))],
        out_specs=pl.BlockSpec((tm, tn), lambda j, i: (i, j)),
        out_shape=jax.ShapeDtypeStruct((m, n), F32),
        compiler_params=_cparams(("arbitrary", "arbitrary")),
        name="matmul",
    )(x, w)


def _ln_rows(z, g, b):
    mu = jnp.mean(z, axis=-1, keepdims=True)
    zc = z - mu
    var = jnp.mean(zc * zc, axis=-1, keepdims=True)
    return zc * lax.rsqrt(var + LN_EPS) * g + b


def _proj_ln_kernel(*refs, n_in, alpha):
    x_ref = refs[0]
    a_refs = refs[1:1 + n_in]
    w_refs = refs[1 + n_in:1 + 2 * n_in]
    g_ref, b_ref, o_ref = refs[1 + 2 * n_in:]
    acc = alpha * x_ref[...]
    for a_ref, w_ref in zip(a_refs, w_refs):
        acc = acc + jnp.dot(a_ref[...].astype(BF16), w_ref[...], preferred_element_type=F32)
    o_ref[...] = _ln_rows(acc, g_ref[...], b_ref[...])


def proj_residual_ln(x, acts, weights, g, b, alpha, tm=256):
    m, d = x.shape
    n_in = len(acts)
    in_specs = [pl.BlockSpec((tm, d), lambda i: (i, 0))]
    in_specs += [pl.BlockSpec((tm, a.shape[1]), lambda i: (i, 0)) for a in acts]
    in_specs += [pl.BlockSpec(w.shape, lambda i: (0, 0)) for w in weights]
    in_specs += [pl.BlockSpec((1, d), lambda i: (0, 0))] * 2
    return pl.pallas_call(
        functools.partial(_proj_ln_kernel, n_in=n_in, alpha=alpha),
        grid=(m // tm,),
        in_specs=in_specs,
        out_specs=pl.BlockSpec((tm, d), lambda i: (i, 0)),
        out_shape=jax.ShapeDtypeStruct((m, d), F32),
        compiler_params=_cparams(("arbitrary",)),
        name="proj_residual_ln",
    )(x, *acts, *weights, g.reshape(1, d), b.reshape(1, d))


def _router_kernel(x_ref, w_ref, b_ref, ti_ref, tg_ref, cnt_ref, *, n_exp, top_k):
    i = pl.program_id(0)

    @pl.when(i == 0)
    def _():
        cnt_ref[...] = jnp.zeros_like(cnt_ref)

    logits = jnp.dot(x_ref[...].astype(BF16), w_ref[...].astype(BF16),
                     preferred_element_type=F32) + b_ref[...]
    tm = logits.shape[0]
    lane = lax.broadcasted_iota(jnp.int32, logits.shape, 1)
    work = logits
    firsts, vals, sel = [], [], jnp.zeros(logits.shape, F32)
    for _ in range(top_k):
        mx = jnp.max(work, axis=-1, keepdims=True)
        first = jnp.min(jnp.where(work == mx, lane, n_exp), axis=-1, keepdims=True)
        pick = lane == first
        firsts.append(first)
        vals.append(mx)
        sel = sel + pick.astype(F32)
        work = jnp.where(pick, -jnp.inf, work)
    exps = [jnp.exp(v - vals[0]) for v in vals]
    den = exps[0]
    for e in exps[1:]:
        den = den + e
    row = lax.broadcasted_iota(jnp.int32, (tm, tm), 0)
    col = lax.broadcasted_iota(jnp.int32, (tm, tm), 1)
    tril = jnp.where(row > col, 1.0, 0.0).astype(BF16)
    before = jnp.dot(tril, sel.astype(BF16), preferred_element_type=F32) + cnt_ref[...]
    out_lane = lax.broadcasted_iota(jnp.int32, (tm, LANES), 1)
    ti = jnp.zeros((tm, LANES), jnp.int32)
    tg = jnp.zeros((tm, LANES), F32)
    for k in range(top_k):
        rank_k = jnp.sum(jnp.where(lane == firsts[k], before, 0.0), axis=-1, keepdims=True)
        ti = jnp.where(out_lane == k, firsts[k], ti)
        ti = jnp.where(out_lane == top_k + k, rank_k.astype(jnp.int32), ti)
        tg = jnp.where(out_lane == k, exps[k] / den, tg)
    ti_ref[...] = ti
    tg_ref[...] = tg
    cnt_ref[...] = cnt_ref[...] + jnp.sum(sel, axis=0, keepdims=True)


def moe_router(x, w_router, b_router):
    n, d = x.shape
    n_exp = w_router.shape[1]
    tm = min(ROUTER_TM, n)
    return pl.pallas_call(
        functools.partial(_router_kernel, n_exp=n_exp, top_k=TOP_K),
        grid=(n // tm,),
        in_specs=[pl.BlockSpec((tm, d), lambda i: (i, 0)),
                  pl.BlockSpec((d, n_exp), lambda i: (0, 0)),
                  pl.BlockSpec((1, n_exp), lambda i: (0, 0))],
        out_specs=[pl.BlockSpec((tm, LANES), lambda i: (i, 0)),
                   pl.BlockSpec((tm, LANES), lambda i: (i, 0)),
                   pl.BlockSpec((1, n_exp), lambda i: (0, 0))],
        out_shape=[jax.ShapeDtypeStruct((n, LANES), jnp.int32),
                   jax.ShapeDtypeStruct((n, LANES), F32),
                   jax.ShapeDtypeStruct((1, n_exp), F32)],
        compiler_params=_cparams(("arbitrary",)),
        name="moe_router",
    )(x, w_router, b_router.reshape(1, n_exp))


def _dispatch_kernel(dest_ref, pad_start_ref, pad_cnt_ref, x_ref, xs_hbm, zero_ref, sem, zsem,
                     *, top_k, n_exp, tb):
    i = pl.program_id(0)
    base = i * (tb * top_k)

    def row_copy(r, k):
        return pltpu.make_async_copy(x_ref.at[pl.ds(r, 1)],
                                     xs_hbm.at[pl.ds(dest_ref[base + r * top_k + k], 1)], sem)

    def for_rows(fn):
        def body(g, c):
            for u in range(DMA_UNROLL):
                for k in range(top_k):
                    fn(row_copy(g * DMA_UNROLL + u, k))
            return c
        lax.fori_loop(0, tb // DMA_UNROLL, body, 0)

    for_rows(lambda cp: cp.start())

    @pl.when(i == 0)
    def _():
        zero_ref[...] = jnp.zeros_like(zero_ref)

        def pad_copy(e, r):
            return pltpu.make_async_copy(zero_ref.at[pl.ds(0, 1)],
                                         xs_hbm.at[pl.ds(pad_start_ref[e] + r, 1)], zsem)

        def pad_body(e, carry):
            def issue(r, cc):
                pad_copy(e, r).start()
                return cc
            lax.fori_loop(0, pad_cnt_ref[e], issue, 0)

            def drain(r, cc):
                pad_copy(e, r).wait()
                return cc
            lax.fori_loop(0, pad_cnt_ref[e], drain, 0)
            return carry

        lax.fori_loop(0, n_exp, pad_body, 0)

    pltpu.make_async_copy(xs_hbm.at[pl.ds(0, tb * top_k)], xs_hbm.at[pl.ds(0, tb * top_k)], sem).wait()


def moe_dispatch(x, dest, pad_start, pad_cnt, n_rows):
    n, d = x.shape
    n_exp = pad_start.shape[0]
    tb = min(DISPATCH_TB, n)
    return pl.pallas_call(
        functools.partial(_dispatch_kernel, top_k=TOP_K, n_exp=n_exp, tb=tb),
        grid_spec=pltpu.PrefetchScalarGridSpec(
            num_scalar_prefetch=3,
            grid=(n // tb,),
            in_specs=[pl.BlockSpec((tb, d), lambda i, de, ps, pc: (i, 0))],
            out_specs=pl.BlockSpec(memory_space=pl.ANY),
            scratch_shapes=[pltpu.VMEM((8, d), F32),
                            pltpu.SemaphoreType.DMA(()),
                            pltpu.SemaphoreType.DMA(())],
        ),
        out_shape=jax.ShapeDtypeStruct((n_rows, d), F32),
        compiler_params=_cparams(("arbitrary",)),
        name="moe_dispatch",
    )(dest, pad_start, pad_cnt, x)


def _expert_kernel(blk_e_ref, blk_rows_ref, blk_src_ref,
                   x_ref, wg_ref, wu_ref, bg_ref, bu_ref, wd_ref, bd_ref, o_ref,
                   xb_s, h_s, wg_s, wu_s, wd_s, *, n_f):
    s = pl.program_id(0)
    t = pl.program_id(1)
    rows = blk_rows_ref[s]
    n_sub = MOE_SUPER // MOE_SUB

    def for_sub_blocks(fn):
        for r in range(n_sub):
            @pl.when(r * MOE_SUB < rows)
            def _():
                fn(pl.ds(r * MOE_SUB, MOE_SUB))

    @pl.when((rows > 0) & (t == 0))
    def _():
        def cast_x(sl):
            xb_s[sl, :] = x_ref[sl, :].astype(BF16)
        for_sub_blocks(cast_x)

    @pl.when((rows > 0) & (t < n_f))
    def _():
        wg_s[...] = wg_ref[0, 0].astype(BF16)
        wu_s[...] = wu_ref[0, 0].astype(BF16)

        def up_proj(sl):
            xb = xb_s[sl, :]
            hg = jnp.dot(xb, wg_s[...], preferred_element_type=F32) + bg_ref[0, 0]
            hu = jnp.dot(xb, wu_s[...], preferred_element_type=F32) + bu_ref[0, 0]
            gl = jnp.minimum(hg, SWIGLU_LIMIT)
            up = jnp.clip(hu, -SWIGLU_LIMIT, SWIGLU_LIMIT)
            h_s[t, sl, :] = (gl * jax.nn.sigmoid(SWIGLU_ALPHA * gl) * (up + 1.0)).astype(BF16)
        for_sub_blocks(up_proj)

    @pl.when((rows > 0) & (t >= n_f))
    def _():
        wd_s[...] = wd_ref[0, 0].astype(BF16)

        def down_proj(sl):
            h = jnp.concatenate([h_s[j, sl, :] for j in range(n_f)], axis=1)
            o_ref[sl, :] = jnp.dot(h, wd_s[...], preferred_element_type=F32) + bd_ref[0, 0]
        for_sub_blocks(down_proj)

        for r in range(1, n_sub):
            @pl.when(r * MOE_SUB >= rows)
            def _():
                o_ref[pl.ds(r * MOE_SUB, MOE_SUB), :] = jnp.zeros((MOE_SUB, o_ref.shape[1]), F32)


def moe_experts(xs, blk_e, blk_rows, blk_src, w_gu, b_gu, w_dn, b_dn, layer):
    n_rows, d = xs.shape
    n_layers, n_exp, _, two_f = w_gu.shape
    d_ff = two_f // 2
    n_super = n_rows // MOE_SUPER
    n_f = d_ff // MOE_TF
    n_d = d // MOE_TN

    def j1(s, t, rows):
        return jnp.where(rows[s] > 0, jnp.minimum(t, n_f - 1), n_f - 1)

    def j2(s, t, rows):
        return jnp.where(rows[s] > 0, jnp.maximum(t - n_f, 0), n_d - 1)

    return pl.pallas_call(
        functools.partial(_expert_kernel, n_f=n_f),
        grid_spec=pltpu.PrefetchScalarGridSpec(
            num_scalar_prefetch=3,
            grid=(n_super, n_f + n_d),
            in_specs=[
                pl.BlockSpec((MOE_SUPER, d), lambda s, t, be, br, bs: (bs[s], 0)),
                pl.BlockSpec((1, 1, d, MOE_TF),
                             lambda s, t, be, br, bs: (layer, be[s], 0, j1(s, t, br))),
                pl.BlockSpec((1, 1, d, MOE_TF),
                             lambda s, t, be, br, bs: (layer, be[s], 0, n_f + j1(s, t, br))),
                pl.BlockSpec((1, 1, 1, MOE_TF),
                             lambda s, t, be, br, bs: (layer, be[s], 0, j1(s, t, br))),
                pl.BlockSpec((1, 1, 1, MOE_TF),
                             lambda s, t, be, br, bs: (layer, be[s], 0, n_f + j1(s, t, br))),
                pl.BlockSpec((1, 1, d_ff, MOE_TN),
                             lambda s, t, be, br, bs: (layer, be[s], 0, j2(s, t, br))),
                pl.BlockSpec((1, 1, 1, MOE_TN),
                             lambda s, t, be, br, bs: (layer, be[s], 0, j2(s, t, br))),
            ],
            out_specs=pl.BlockSpec((MOE_SUPER, MOE_TN), lambda s, t, be, br, bs: (bs[s], j2(s, t, br))),
            scratch_shapes=[pltpu.VMEM((MOE_SUPER, d), BF16),
                            pltpu.VMEM((n_f, MOE_SUPER, MOE_TF), BF16),
                            pltpu.VMEM((d, MOE_TF), BF16),
                            pltpu.VMEM((d, MOE_TF), BF16),
                            pltpu.VMEM((d_ff, MOE_TN), BF16)],
        ),
        out_shape=jax.ShapeDtypeStruct((n_rows, d), F32),
        compiler_params=_cparams(("arbitrary", "arbitrary")),
        name="moe_experts",
    )(blk_e, blk_rows, blk_src, xs, w_gu, w_gu, b_gu.reshape(n_layers, n_exp, 1, two_f),
      b_gu.reshape(n_layers, n_exp, 1, two_f), w_dn, b_dn.reshape(n_layers, n_exp, 1, d))


def _combine_kernel(dest_ref, x_ref, tg_ref, g_ref, b_ref, yb_hbm, o_ref, buf, sem,
                    *, top_k, alpha, tb):
    i = pl.program_id(0)
    n_blk = pl.num_programs(0)

    def start_rows(blk, slot):
        base = blk * (tb * top_k)

        def body(g, c):
            for u in range(DMA_UNROLL):
                r = g * DMA_UNROLL + u
                for k in range(top_k):
                    pltpu.make_async_copy(
                        yb_hbm.at[pl.ds(dest_ref[base + r * top_k + k], 1)],
                        buf.at[slot, pl.ds(k * tb + r, 1)], sem.at[slot]).start()
            return c
        lax.fori_loop(0, tb // DMA_UNROLL, body, 0)

    @pl.when(i == 0)
    def _():
        start_rows(0, 0)

    @pl.when(i + 1 < n_blk)
    def _():
        start_rows(i + 1, (i + 1) % 2)

    slot = i % 2
    pltpu.make_async_copy(yb_hbm.at[pl.ds(0, tb * top_k)], buf.at[slot], sem.at[slot]).wait()

    tg = tg_ref[...]
    acc = alpha * x_ref[...]
    for k in range(top_k):
        acc = acc + tg[:, k:k + 1] * buf[slot, k * tb:(k + 1) * tb]
    o_ref[...] = _ln_rows(acc, g_ref[...], b_ref[...])


def moe_combine_ln(x, yb, dest, tg, g, b, alpha):
    n, d = x.shape
    tb = COMBINE_TB
    return pl.pallas_call(
        functools.partial(_combine_kernel, top_k=TOP_K, alpha=alpha, tb=tb),
        grid_spec=pltpu.PrefetchScalarGridSpec(
            num_scalar_prefetch=1,
            grid=(n // tb,),
            in_specs=[pl.BlockSpec((tb, d), lambda i, dr: (i, 0)),
                      pl.BlockSpec((tb, LANES), lambda i, dr: (i, 0)),
                      pl.BlockSpec((1, d), lambda i, dr: (0, 0)),
                      pl.BlockSpec((1, d), lambda i, dr: (0, 0)),
                      pl.BlockSpec(memory_space=pl.ANY)],
            out_specs=pl.BlockSpec((tb, d), lambda i, dr: (i, 0)),
            scratch_shapes=[pltpu.VMEM((2, TOP_K * tb, d), F32),
                            pltpu.SemaphoreType.DMA((2,))],
        ),
        out_shape=jax.ShapeDtypeStruct((n, d), F32),
        compiler_params=_cparams(("arbitrary",)),
        name="moe_combine_ln",
    )(dest, x, tg, g.reshape(1, d), b.reshape(1, d), yb)


def moe_layer(x, w_router, b_router, w_gu, b_gu, w_dn, b_dn, layer, ln_g, ln_b, alpha):
    n, d = x.shape
    n_exp = w_router.shape[1]
    m = n * TOP_K
    n_super = -(-m // MOE_SUPER) + n_exp
    n_rows = n_super * MOE_SUPER

    ti, tg, cnt = moe_router(x, w_router, b_router)
    counts = cnt[0].astype(jnp.int32)
    nsup = (counts + MOE_SUPER - 1) // MOE_SUPER
    sup_end = jnp.cumsum(nsup)
    sup_start = sup_end - nsup
    row_start = sup_start * MOE_SUPER
    top_i = ti[:, :TOP_K]
    rank = ti[:, TOP_K:2 * TOP_K]
    dest = (row_start[top_i] + rank).reshape(m)
    s_idx = jnp.arange(n_super, dtype=jnp.int32)
    n_used = sup_end[-1]
    used = s_idx < n_used
    src = jnp.where(used, s_idx, n_used - 1)
    blk_e = jnp.minimum(jnp.searchsorted(sup_end, src, side="right"), n_exp - 1).astype(jnp.int32)
    blk_rows = jnp.clip(counts[blk_e] - (src - sup_start[blk_e]) * MOE_SUPER, 0, MOE_SUPER)
    blk_rows = jnp.where(used, blk_rows, 0).astype(jnp.int32)
    pad_start = row_start + counts
    pad_cnt = (-counts) % MOE_SUB

    xs = moe_dispatch(x, dest, pad_start, pad_cnt, n_rows)
    yb = moe_experts(xs, blk_e, blk_rows, src.astype(jnp.int32), w_gu, b_gu, w_dn, b_dn, layer)
    return moe_combine_ln(x, yb, dest, tg, ln_g, ln_b, alpha)


def _head_sum_matrix(n_b, scale):
    row = lax.broadcasted_iota(jnp.int32, (LANES, LANES), 0)
    col = lax.broadcasted_iota(jnp.int32, (LANES, LANES), 1)
    return jnp.where(row // n_b == col // n_b, scale, 0.0).astype(BF16)


def _split_dot(dot_fn, x, terms=3):
    acc = None
    for _ in range(terms):
        piece = x.astype(BF16)
        part = dot_fn(piece)
        acc = part if acc is None else acc + part
        x = x - piece.astype(F32)
    return acc


def _per_head(x, mat):
    outs = [_split_dot(lambda a: jnp.dot(a, mat, preferred_element_type=F32),
                       x[:, s * LANES:(s + 1) * LANES]) for s in range(x.shape[1] // LANES)]
    return jnp.concatenate(outs, axis=1)


def _rwkv_prep_kernel(p_ref, prev_ref, mu_ref, w0_ref, w2_ref, a0_ref, a2_ref, g2_ref, kk_ref,
                      ka_ref, r_ref, k_ref, v_ref, lw_ref, al_ref, be_ref, g_ref,
                      *, d_b, wl, al, n_b):
    p = p_ref[...]
    xm = p + (prev_ref[...] - p) * mu_ref[...]
    r = xm[:, :d_b]
    k = xm[:, d_b:2 * d_b]
    v = xm[:, 2 * d_b:3 * d_b]
    wd = xm[:, 3 * d_b:3 * d_b + wl]
    ad = xm[:, 3 * d_b + wl:3 * d_b + wl + al]
    gd = xm[:, 3 * d_b + wl + al:]
    dotb = lambda a, w_ref: jnp.dot(a.astype(BF16), w_ref[...].astype(BF16),
                                    preferred_element_type=F32)
    wlog = -jax.nn.softplus(-(w0_ref[...] + dotb(jnp.tanh(wd), w2_ref))) - 0.5
    a = jax.nn.sigmoid(a0_ref[...] + dotb(ad, a2_ref))
    kk = k * kk_ref[...]
    norm = jnp.sqrt(_per_head(kk * kk, _head_sum_matrix(n_b, 1.0)))
    kk = kk / jnp.maximum(norm, 1e-12)
    r_ref[...] = r
    k_ref[...] = k * (1.0 + (a - 1.0) * ka_ref[...])
    v_ref[...] = v
    lw_ref[...] = -jnp.exp(wlog)
    al_ref[...] = -kk
    be_ref[...] = kk * a
    g_ref[...] = dotb(jax.nn.sigmoid(gd), g2_ref)


def rwkv_prep(p, row0, prev, mu, w0, w2, a0, a2, g2, k_k, k_a, n_b, tm=256):
    n, pb = prev.shape
    blk0 = row0 // tm
    d_b = w0.shape[0]
    wl, al = w2.shape[0], a2.shape[0]
    row = lambda c: pl.BlockSpec((tm, c), lambda i: (i, 0))
    full = lambda a: pl.BlockSpec(a.shape, lambda i: (0, 0))
    vec = lambda a: a.reshape(1, -1)
    consts = [vec(mu), vec(w0), w2, vec(a0), a2, g2, vec(k_k), vec(k_a)]
    return pl.pallas_call(
        functools.partial(_rwkv_prep_kernel, d_b=d_b, wl=wl, al=al, n_b=n_b),
        grid=(n // tm,),
        in_specs=[pl.BlockSpec((tm, pb), lambda i: (blk0 + i, 0)), row(pb)] + [full(c) for c in consts],
        out_specs=[row(d_b)] * 7,
        out_shape=[jax.ShapeDtypeStruct((n, d_b), F32)] * 7,
        compiler_params=_cparams(("arbitrary",)),
        name="rwkv_prep",
    )(p, prev, *consts)


def _rwkv_chunk_kernel(r_ref, k_ref, v_ref, lw_ref, al_ref, be_ref, g_ref, s0_ref, gng_ref,
                       gnb_ref, rk_ref, o_ref, s_out_ref, st_ref, *, n_b, chunk, seg, n_pairs, n_par):
    c = pl.program_id(1)
    c2 = 2 * chunk
    n_seg = chunk // seg

    @pl.when(c == 0)
    def _():
        st_ref[...] = s0_ref[...]

    left = lambda mat: (lambda a: jnp.dot(mat, a, preferred_element_type=F32))
    right = lambda mat: (lambda a: jnp.dot(a, mat, preferred_element_type=F32))
    dot = lambda a, b: jnp.dot(a.astype(BF16), b.astype(BF16), preferred_element_type=F32)
    dot_nt = lambda a, b: lax.dot_general(a.astype(BF16), b.astype(BF16), (((1,), (1,)), ((), ())),
                                          preferred_element_type=F32)
    dot_tn = lambda a, b: lax.dot_general(a.astype(BF16), b.astype(BF16), (((0,), (0,)), ((), ())),
                                          preferred_element_type=F32)
    lane = lax.broadcasted_iota(jnp.int32, (chunk, LANES), 1)
    head_a = lane < n_b
    stack = lambda x: jnp.concatenate([jnp.where(head_a, x, 0.0), jnp.where(head_a, 0.0, x)], axis=0)
    twice = lambda x: jnp.concatenate([x, x], axis=0)
    row2 = lax.broadcasted_iota(jnp.int32, (c2, c2), 0)
    col2 = lax.broadcasted_iota(jnp.int32, (c2, c2), 1)
    same = (row2 // seg) == (col2 // seg)
    strict = same & (row2 > col2)
    incl = same & (row2 >= col2)
    rowc = lax.broadcasted_iota(jnp.int32, (chunk, chunk), 0)
    colc = lax.broadcasted_iota(jnp.int32, (chunk, chunk), 1)
    cum = jnp.where((rowc >= colc) & (rowc // seg == colc // seg), 1.0, 0.0).astype(BF16)
    rl = lax.broadcasted_iota(jnp.int32, (LANES, LANES), 0)
    cl_ = lax.broadcasted_iota(jnp.int32, (LANES, LANES), 1)
    blockdiag = (rl // n_b) == (cl_ // n_b)
    sum_mat = _head_sum_matrix(n_b, 1.0)
    segs = [slice(q * seg, (q + 1) * seg) for q in range(n_seg)]
    cat0 = lambda parts: parts[0] if len(parts) == 1 else jnp.concatenate(parts, axis=0)

    cl_all = [_split_dot(left(cum), lw_ref[bi]) for bi in range(n_par)]
    ys_all, stat_rows = [], []

    for bi, p in [(bi, p) for bi in range(n_par) for p in range(n_pairs)]:
        sl = slice(p * LANES, (p + 1) * LANES)
        r, k, v = r_ref[bi, :, sl], k_ref[bi, :, sl], v_ref[bi, :, sl]
        lw, al, be = lw_ref[bi, :, sl], al_ref[bi, :, sl], be_ref[bi, :, sl]
        cl = cl_all[bi][:, sl]
        e_neg = jnp.exp(-cl)
        at = al * jnp.exp(cl - lw)
        rt = r * jnp.exp(cl)
        bt = be * e_neg
        kt = k * e_neg
        gram = dot_nt(jnp.concatenate([stack(at), stack(rt)], axis=0),
                      jnp.concatenate([twice(bt), twice(kt)], axis=0))
        nm = jnp.where(strict, gram[:c2, :c2], 0.0)
        aak = jnp.where(strict, gram[:c2, c2:], 0.0)
        arb = jnp.where(incl, gram[c2:, :c2], 0.0)
        ark = jnp.where(incl, gram[c2:, c2:], 0.0)
        ms = [st_ref[bi * n_seg + q, p] for q in range(n_seg)]
        fs = [dot(jnp.concatenate([at[sq], rt[sq]], axis=0), ms[q]) for q, sq in enumerate(segs)]
        a_s0 = cat0([f[:seg] for f in fs])
        r_s0 = cat0([f[seg:] for f in fs])
        vs = stack(v)
        u = stack(a_s0) + dot(aak, vs)
        pw = nm
        n = 1
        while n < seg:
            u = u + dot(pw, u)
            n *= 2
            if n < seg:
                pw = dot(pw, pw)
        ys = stack(r_s0) + dot(jnp.concatenate([arb, ark], axis=1), jnp.concatenate([u, vs], axis=0))
        y = ys[:chunk] + ys[chunk:]
        up = u[:chunk] + u[chunk:]
        ends = [cl[sq.stop - 1:sq.stop, :] for sq in segs]
        to_end = jnp.exp(cat0([jnp.broadcast_to(e, (seg, LANES)) for e in ends]) - cl)
        b_end, k_end = be * to_end, k * to_end
        for q, sq in enumerate(segs):
            upd = dot_tn(jnp.concatenate([b_end[sq], k_end[sq]], axis=0),
                         jnp.concatenate([up[sq], v[sq]], axis=0))
            pc_col = jnp.transpose(jnp.broadcast_to(jnp.exp(ends[q]), (8, LANES)))[:, 0:1]
            st_ref[bi * n_seg + q, p] = pc_col * ms[q] + jnp.where(blockdiag, upd, 0.0)

        ys_all.append((bi, sl, y))
        stat_rows += [y, y * y, r * k * rk_ref[:, sl]]

    stats = _split_dot(right(sum_mat), jnp.concatenate(stat_rows, axis=0))
    for i, (bi, sl, y) in enumerate(ys_all):
        st3 = stats[3 * i * chunk:3 * (i + 1) * chunk]
        mean = st3[:chunk] * (1.0 / n_b)
        var = st3[chunk:c2] * (1.0 / n_b) - mean * mean
        yn = (y - mean) * lax.rsqrt(var + RWKV_GN_EPS) * gng_ref[:, sl] + gnb_ref[:, sl]
        o_ref[bi, :, sl] = (yn + st3[c2:] * v_ref[bi, :, sl]) * g_ref[bi, :, sl]

    @pl.when(c == pl.num_programs(1) - 1)
    def _():
        s_out_ref[...] = st_ref[...]


def rwkv_chunked(r, k, v, lw, al, be, g, s0_bd, gn_g, gn_b, r_k, bsz, t, n_b, chunk):
    n, d_b = r.shape
    n_pairs = d_b // LANES
    seg = min(t, chunk)
    n_seg = chunk // seg
    n_c = t // seg
    if n_seg == 1 and bsz % RWKV_SEQS_PER_STEP == 0:
        n_par = RWKV_SEQS_PER_STEP
        view = (bsz, t, d_b)
        row = pl.BlockSpec((n_par, chunk, d_b), lambda b, c: (b, c, 0))
    else:
        n_par = 1
        view = (1, n, d_b)
        row = pl.BlockSpec((1, chunk, d_b), lambda b, c: (0, b * n_c + c, 0))
    vec = pl.BlockSpec((1, d_b), lambda b, c: (0, 0))
    n_st = n_par * n_seg
    st = pl.BlockSpec((n_st, n_pairs, LANES, LANES), lambda b, c: (b, 0, 0, 0))
    o, m = pl.pallas_call(
        functools.partial(_rwkv_chunk_kernel, n_b=n_b, chunk=chunk, seg=seg, n_pairs=n_pairs,
                          n_par=n_par),
        grid=(bsz // n_st, n_c),
        in_specs=[row] * 7 + [st, vec, vec, vec],
        out_specs=[row, st],
        out_shape=[jax.ShapeDtypeStruct(view, F32), jax.ShapeDtypeStruct(s0_bd.shape, F32)],
        scratch_shapes=[pltpu.VMEM((n_st, n_pairs, LANES, LANES), F32)],
        compiler_params=_cparams(("arbitrary", "arbitrary")),
        name="rwkv_chunked",
    )(*[a.reshape(view) for a in (r, k, v, lw, al, be, g)], s0_bd, gn_g.reshape(1, d_b),
      gn_b.reshape(1, d_b), r_k.reshape(1, d_b))
    return o.reshape(n, d_b), m


def _state_to_blockdiag(s):
    bsz, n_h, n_v, n_k = s.shape
    st = jnp.swapaxes(s, 2, 3).reshape(bsz, n_h // 2, 2, n_k, n_v)
    z = jnp.zeros_like(st[:, :, 0])
    top = jnp.concatenate([st[:, :, 0], z], axis=-1)
    bot = jnp.concatenate([z, st[:, :, 1]], axis=-1)
    return jnp.concatenate([top, bot], axis=-2)


def _blockdiag_to_state(m, n_b):
    bsz, n_pairs = m.shape[:2]
    a = m[:, :, :n_b, :n_b]
    b = m[:, :, n_b:, n_b:]
    st = jnp.stack([a, b], axis=2).reshape(bsz, n_pairs * 2, n_b, n_b)
    return jnp.swapaxes(st, 2, 3)


def rwkv7_group(p_all, row0, bsz, t, shift_prev, s0, mu, w0, w2, a0, a2, g2, k_k, k_a, r_k, gn_g,
                gn_b, chunk):
    pb = p_all.shape[1]
    n_b = s0.shape[2]
    p = p_all[row0:row0 + bsz * t].reshape(bsz, t, pb)
    prev = jnp.concatenate([shift_prev[:, None], p[:, :-1]], axis=1).reshape(bsz * t, pb)
    tm = 256
    r, k, v, lw, al, be, g = rwkv_prep(p_all, row0, prev, mu, w0, w2, a0, a2, g2, k_k, k_a, n_b, tm=tm)
    o, m = rwkv_chunked(r, k, v, lw, al, be, g, _state_to_blockdiag(s0), gn_g, gn_b, r_k,
                        bsz, t, n_b, chunk)
    return o, _blockdiag_to_state(m, n_b), p[:, -1]


def _hgrn_kernel(q_ref, f_ref, i_ref, gate_ref, lb_ref, ng_ref, s0_ref, o_ref, s_out_ref, st_ref,
                 *, n_h, sub, tb):
    c = pl.program_id(1)

    @pl.when(c == 0)
    def _():
        st_ref[...] = s0_ref[0]

    row = lax.broadcasted_iota(jnp.int32, (sub, sub), 0)
    col = lax.broadcasted_iota(jnp.int32, (sub, sub), 1)
    cum = jnp.where(row >= col, 1.0, 0.0).astype(BF16)
    t_idx = lax.broadcasted_iota(jnp.int32, (sub, LANES), 0)
    bf = lambda x: x.astype(BF16)

    def body(j, h0):
        rows = pl.ds(pl.multiple_of(j * sub, sub), sub)
        cols = slice(h0 * LANES, (h0 + HGRN_HEAD_GROUP) * LANES)
        lb_g = lb_ref[:, cols]
        z_g = f_ref[rows, cols]
        logf_g = jnp.log(lb_g + (1.0 - lb_g) * jax.nn.sigmoid(z_g))
        b_g = _split_dot(lambda a: jnp.dot(cum, a, preferred_element_type=F32), logf_g)
        kk_g = (1.0 - lb_g) * jax.nn.sigmoid(-z_g)
        for h in range(h0, h0 + HGRN_HEAD_GROUP):
            sl = slice(h * LANES, (h + 1) * LANES)
            gl = slice((h - h0) * LANES, (h - h0 + 1) * LANES)
            s_mat = st_ref[h]
            q = jax.nn.silu(q_ref[rows, sl])
            kk = kk_g[:, gl]
            v = i_ref[rows, sl]
            b = b_g[:, gl]
            o = jnp.dot(bf(q * jnp.exp(b)), bf(s_mat), preferred_element_type=F32)
            vb = bf(v).astype(F32)
            for s in range(sub):
                dec = jnp.exp(jnp.where(t_idx >= s, b - b[s:s + 1, :], -jnp.inf))
                att = jnp.sum(q * kk[s:s + 1, :] * dec, axis=-1, keepdims=True)
                o = o + bf(att).astype(F32) * vb[s:s + 1, :]
            bl = b[sub - 1:sub, :]
            kd = bf(kk * jnp.exp(bl - b))
            upd = lax.dot_general(kd, bf(v), (((0,), (0,)), ((), ())), preferred_element_type=F32)
            decay = jnp.exp(bl)
            st_ref[h] = jnp.transpose(jnp.broadcast_to(decay, (8, LANES)))[:, 0:1] * s_mat + upd
            o = o * lax.rsqrt(jnp.mean(o * o, axis=-1, keepdims=True) + LN_EPS) * ng_ref[...]
            o_ref[rows, sl] = o * jax.nn.silu(gate_ref[rows, sl])

    for h0 in range(0, n_h, HGRN_HEAD_GROUP):
        def group_body(j, carry, h0=h0):
            body(j, h0)
            return carry
        lax.fori_loop(0, tb // sub, group_body, 0)

    @pl.when(c == pl.num_programs(1) - 1)
    def _():
        s_out_ref[0] = st_ref[...]


def hgrn2(proj, row0, s0, lb, norm_g, bsz, t, out_buf=None):
    n = proj.shape[0]
    n_h, dk, dv = s0.shape[1:]
    d_a = n_h * dk
    sub = min(HGRN_SUB, t)
    tb = min(256, t)
    n_c = t // tb
    blk0 = row0 // tb
    col = lambda which: pl.BlockSpec((tb, d_a), lambda b, c: (blk0 + b * n_c + c, which))
    st = pl.BlockSpec((1, n_h, dk, dv), lambda b, c: (b, 0, 0, 0))
    return _shared_rows_call(
        functools.partial(_hgrn_kernel, n_h=n_h, sub=sub, tb=tb), 7, out_buf,
        grid=(bsz, n_c),
        in_specs=[col(0), col(1), col(2), col(3),
                  pl.BlockSpec((1, d_a), lambda b, c: (0, 0)),
                  pl.BlockSpec((1, dv), lambda b, c: (0, 0)), st],
        out_specs=[col(0), st],
        out_shape=[jax.ShapeDtypeStruct((n, d_a), F32), jax.ShapeDtypeStruct(s0.shape, F32)],
        scratch_shapes=[pltpu.VMEM((n_h, dk, dv), F32)],
        compiler_params=_cparams(("arbitrary", "arbitrary")),
        name="hgrn2",
    )(proj, proj, proj, proj, lb.reshape(1, d_a), norm_g.reshape(1, dv), s0)


def _pool_ln_kernel(x_ref, halo_ref, w_ref, sc_ref, g_ref, b_ref, o_ref, xx_ref,
                    *, windows, start_pos, alpha, tb, halo):
    c = pl.program_id(1)
    d = x_ref.shape[1]
    pc = d // len(windows)
    xx_ref[0:halo, :] = halo_ref[0, 0]
    xx_ref[halo:halo + tb, :] = x_ref[...]
    x = x_ref[...]
    pos = start_pos + c * tb + lax.broadcasted_iota(jnp.int32, (tb, 1), 0)
    ys = []
    for gi, w in enumerate(windows):
        sl = slice(gi * pc, (gi + 1) * pc)
        acc = x[:, sl]
        for back in range(1, w):
            acc = acc + xx_ref[halo - back:halo - back + tb, sl]
        cnt = jnp.minimum(pos + 1, w).astype(F32)
        u = acc / cnt - x[:, sl]
        ys.append(jnp.dot(u.astype(BF16), w_ref[gi], preferred_element_type=F32))
    y = jnp.concatenate(ys, axis=1) * sc_ref[...]
    o_ref[...] = _ln_rows(alpha * x + y, g_ref[...], b_ref[...])


def pool_residual_ln(x, row0, hist, start_pos, w_pool, scale, g, b, alpha, bsz, t, out_buf=None):
    n, d = x.shape
    halo = 16
    tb = min(256, t)
    n_c = t // tb
    blk0 = row0 // tb
    x4 = x[row0:row0 + bsz * t].reshape(bsz, n_c, tb, d)
    first = jnp.concatenate([jnp.zeros((bsz, halo - hist.shape[1], d), F32), hist], axis=1)
    halos = first[:, None]
    if n_c > 1:
        halos = jnp.concatenate([halos, x4[:, :-1, tb - halo:]], axis=1)
    vec = lambda a: pl.BlockSpec((1, d), lambda bi, c: (0, 0))
    rows = pl.BlockSpec((tb, d), lambda bi, c: (blk0 + bi * n_c + c, 0))
    return _shared_rows_call(
        functools.partial(_pool_ln_kernel, windows=POOL_WINDOWS, start_pos=start_pos,
                          alpha=alpha, tb=tb, halo=halo), 6, out_buf,
        grid=(bsz, n_c),
        in_specs=[rows,
                  pl.BlockSpec((1, 1, halo, d), lambda bi, c: (bi, c, 0, 0)),
                  pl.BlockSpec(w_pool.shape, lambda bi, c: (0, 0, 0)),
                  vec(scale), vec(g), vec(b)],
        out_specs=rows,
        out_shape=jax.ShapeDtypeStruct((n, d), F32),
        scratch_shapes=[pltpu.VMEM((halo + tb, d), F32)],
        compiler_params=_cparams(("arbitrary", "arbitrary")),
        name="pool_residual_ln",
    )(x, halos, w_pool.astype(BF16), scale.reshape(1, d), g.reshape(1, d), b.reshape(1, d))


def kernel(x_prompt, x_sample, state_hgrn, state_rwkv, state_rwkv_shift, state_pool, mix_w_in, hgrn_lb, hgrn_norm_g, rwkv_mu, rwkv_w0, rwkv_w2, rwkv_a0, rwkv_a2, rwkv_g2, rwkv_k_k, rwkv_k_a, rwkv_r_k, rwkv_gn_g, rwkv_gn_b, mix_w_out, pool_w, pool_scale, ln1_g, ln1_b, ln2_g, ln2_b, moe_w_router, moe_b_router, moe_w_gu, moe_b_gu, moe_w_dn, moe_b_dn):
    bp, tp, d = x_prompt.shape
    bs, ts, _ = x_sample.shape
    depth = ln1_g.shape[0]
    n_h_a, dk_a, dv_a = state_hgrn.shape[2:]
    n_h_b, n_b = state_rwkv.shape[2:4]
    d_a = n_h_a * dk_a
    pb = state_rwkv_shift.shape[2]
    hist = state_pool.shape[2]
    n_p = bp * tp
    alpha = (2 * depth) ** 0.25

    lb_p = jax.nn.softmax(hgrn_lb, axis=0)
    lb_all = jnp.cumsum(lb_p, axis=0) - lb_p[0]
    x = jnp.concatenate([x_prompt.reshape(n_p, d), x_sample.reshape(bs * ts, d)], axis=0)
    hg_p, hg_s, rw_p, rw_s, sh_p, sh_s, pl_p, pl_s = [], [], [], [], [], [], [], []
    for l in range(depth):
        if l % 2 == 0:
            e = l // 2
            p_a = matmul(x, mix_w_in, e, 0, 4 * d_a)
            p_b = matmul(x, mix_w_in, e, 4 * d_a, pb)
            rw = (rwkv_mu[e], rwkv_w0[e], rwkv_w2[e], rwkv_a0[e], rwkv_a2[e], rwkv_g2[e],
                  rwkv_k_k[e], rwkv_k_a[e], rwkv_r_k[e], rwkv_gn_g[e], rwkv_gn_b[e])
            o_a, o_bs = None, []
            for grp, (lo, hi, bsz, t) in enumerate(((0, n_p, bp, tp), (n_p, n_p + bs * ts, bs, ts))):
                if grp == 0:
                    s_h = jnp.zeros((bsz, n_h_a, dk_a, dv_a), F32)
                    s_r = jnp.zeros((bsz, n_h_b, n_b, n_b), F32)
                    s_s = jnp.zeros((bsz, pb), F32)
                else:
                    s_h, s_r, s_s = state_hgrn[e], state_rwkv[e], state_rwkv_shift[e]
                o_a, n_h = hgrn2(p_a, lo, s_h, lb_all[e], hgrn_norm_g[e], bsz, t, out_buf=o_a)
                o_g, n_r, n_s = rwkv7_group(p_b, lo, bsz, t, s_s, s_r, *rw, chunk=RWKV_CHUNK)
                o_bs.append(o_g)
                (hg_p, hg_s)[grp].append(n_h)
                (rw_p, rw_s)[grp].append(n_r)
                (sh_p, sh_s)[grp].append(n_s)
            o_b = jnp.concatenate(o_bs, axis=0)
            w_out = mix_w_out[e].astype(BF16)
            x = proj_residual_ln(x, [o_a, o_b], [w_out[:d_a], w_out[d_a:]], ln1_g[l], ln1_b[l], alpha)
        else:
            j = l // 2
            x_p, x_s = x[:n_p], x[n_p:]
            hist_p = jnp.zeros((bp, hist, d), F32)
            pl_p.append(jnp.concatenate([hist_p, x_p.reshape(bp, tp, d)], axis=1)[:, -hist:])
            pl_s.append(jnp.concatenate([state_pool[j], x_s.reshape(bs, ts, d)], axis=1)[:, -hist:])
            pool = (pool_w[j], pool_scale[j], ln1_g[l], ln1_b[l], alpha)
            x_new = pool_residual_ln(x, 0, hist_p, 0, *pool, bp, tp)
            x = pool_residual_ln(x, n_p, state_pool[j], PAST_LEN, *pool, bs, ts, out_buf=x_new)
        x = moe_layer(x, moe_w_router[l], moe_b_router[l], moe_w_gu, moe_b_gu, moe_w_dn, moe_b_dn, l,
                      ln2_g[l], ln2_b[l], alpha)
    return (x[:n_p].reshape(bp, tp, d), x[n_p:].reshape(bs, ts, d),
            jnp.stack(hg_p), jnp.stack(hg_s), jnp.stack(rw_p), jnp.stack(rw_s),
            jnp.stack(sh_p), jnp.stack(sh_s), jnp.stack(pl_p), jnp.stack(pl_s))
```

```python
import functools

import jax
import jax.numpy as jnp
from jax import lax
from jax.experimental import pallas as pl
from jax.experimental.pallas import tpu as pltpu

F32 = jnp.float32
BF16 = jnp.bfloat16

POOL_WINDOWS = (2, 4, 8, 16)
TOP_K = 4
SWIGLU_LIMIT = 7.0
SWIGLU_ALPHA = 1.702
LN_EPS = 1e-5
RWKV_GN_EPS = 64e-5
PAST_LEN = 16384

VMEM_LIMIT_BYTES = 56 * 1024 * 1024
LANES = 128

MOE_SUPER = 1280
MOE_SUB = 640
MOE_TF = 256
MOE_TN = 256
ROUTER_TM = 512
COMBINE_TB = 128
HGRN_SUB = 16
HGRN_HEAD_GROUP = 8
RWKV_SEQS_PER_STEP = 1
RWKV_CHUNK = 64
DISPATCH_TB = 256
DMA_UNROLL = 8


def _cparams(sem):
    return pltpu.CompilerParams(dimension_semantics=sem, vmem_limit_bytes=VMEM_LIMIT_BYTES)


def _shared_rows_call(kernel, n_in, out_buf, **kwargs):
    if out_buf is None:
        return pl.pallas_call(kernel, **kwargs)
    body = lambda *refs: kernel(*refs[:n_in], *refs[n_in + 1:])
    kwargs["in_specs"] = list(kwargs["in_specs"]) + [pl.BlockSpec(memory_space=pl.ANY)]
    call = pl.pallas_call(body, input_output_aliases={n_in: 0}, **kwargs)
    return lambda *args: call(*args, out_buf)


def _mm_kernel(x_ref, w_ref, o_ref):
    o_ref[...] = jnp.dot(x_ref[...].astype(BF16), w_ref[0].astype(BF16),
                         preferred_element_type=F32)


def matmul(x, w, layer, col0, n, tm=512, tn=1024):
    m, k = x.shape
    tm = min(tm, m)
    assert col0 % tn == 0
    c0 = col0 // tn
    return pl.pallas_call(
        _mm_kernel,
        grid=(pl.cdiv(n, tn), pl.cdiv(m, tm)),
        in_specs=[pl.BlockSpec((tm, k), lambda j, i: (i, 0)),
                  pl.BlockSpec((1, k, tn), lambda j, i: (layer, 0, c0 + j))],
        out_specs=pl.BlockSpec((tm, tn), lambda j, i: (i, j)),
        out_shape=jax.ShapeDtypeStruct((m, n), F32),
        compiler_params=_cparams(("arbitrary", "arbitrary")),
        name="matmul",
    )(x, w)


def _ln_rows(z, g, b):
    mu = jnp.mean(z, axis=-1, keepdims=True)
    zc = z - mu
    var = jnp.mean(zc * zc, axis=-1, keepdims=True)
    return zc * lax.rsqrt(var + LN_EPS) * g + b


def _proj_ln_kernel(*refs, n_in, alpha):
    x_ref = refs[0]
    a_refs = refs[1:1 + n_in]
    w_refs = refs[1 + n_in:1 + 2 * n_in]
    g_ref, b_ref, o_ref = refs[1 + 2 * n_in:]
    acc = alpha * x_ref[...]
    for a_ref, w_ref in zip(a_refs, w_refs):
        acc = acc + jnp.dot(a_ref[...].astype(BF16), w_ref[...], preferred_element_type=F32)
    o_ref[...] = _ln_rows(acc, g_ref[...], b_ref[...])


def proj_residual_ln(x, acts, weights, g, b, alpha, tm=256):
    m, d = x.shape
    n_in = len(acts)
    in_specs = [pl.BlockSpec((tm, d), lambda i: (i, 0))]
    in_specs += [pl.BlockSpec((tm, a.shape[1]), lambda i: (i, 0)) for a in acts]
    in_specs += [pl.BlockSpec(w.shape, lambda i: (0, 0)) for w in weights]
    in_specs += [pl.BlockSpec((1, d), lambda i: (0, 0))] * 2
    return pl.pallas_call(
        functools.partial(_proj_ln_kernel, n_in=n_in, alpha=alpha),
        grid=(m // tm,),
        in_specs=in_specs,
        out_specs=pl.BlockSpec((tm, d), lambda i: (i, 0)),
        out_shape=jax.ShapeDtypeStruct((m, d), F32),
        compiler_params=_cparams(("arbitrary",)),
        name="proj_residual_ln",
    )(x, *acts, *weights, g.reshape(1, d), b.reshape(1, d))


def _router_kernel(x_ref, w_ref, b_ref, ti_ref, tg_ref, cnt_ref, *, n_exp, top_k):
    i = pl.program_id(0)

    @pl.when(i == 0)
    def _():
        cnt_ref[...] = jnp.zeros_like(cnt_ref)

    logits = jnp.dot(x_ref[...].astype(BF16), w_ref[...].astype(BF16),
                     preferred_element_type=F32) + b_ref[...]
    tm = logits.shape[0]
    lane = lax.broadcasted_iota(jnp.int32, logits.shape, 1)
    work = logits
    firsts, vals, sel = [], [], jnp.zeros(logits.shape, F32)
    for _ in range(top_k):
        mx = jnp.max(work, axis=-1, keepdims=True)
        first = jnp.min(jnp.where(work == mx, lane, n_exp), axis=-1, keepdims=True)
        pick = lane == first
        firsts.append(first)
        vals.append(mx)
        sel = sel + pick.astype(F32)
        work = jnp.where(pick, -jnp.inf, work)
    exps = [jnp.exp(v - vals[0]) for v in vals]
    den = exps[0]
    for e in exps[1:]:
        den = den + e
    row = lax.broadcasted_iota(jnp.int32, (tm, tm), 0)
    col = lax.broadcasted_iota(jnp.int32, (tm, tm), 1)
    tril = jnp.where(row > col, 1.0, 0.0).astype(BF16)
    before = jnp.dot(tril, sel.astype(BF16), preferred_element_type=F32) + cnt_ref[...]
    out_lane = lax.broadcasted_iota(jnp.int32, (tm, LANES), 1)
    ti = jnp.zeros((tm, LANES), jnp.int32)
    tg = jnp.zeros((tm, LANES), F32)
    for k in range(top_k):
        rank_k = jnp.sum(jnp.where(lane == firsts[k], before, 0.0), axis=-1, keepdims=True)
        ti = jnp.where(out_lane == k, firsts[k], ti)
        ti = jnp.where(out_lane == top_k + k, rank_k.astype(jnp.int32), ti)
        tg = jnp.where(out_lane == k, exps[k] / den, tg)
    ti_ref[...] = ti
    tg_ref[...] = tg
    cnt_ref[...] = cnt_ref[...] + jnp.sum(sel, axis=0, keepdims=True)


def moe_router(x, w_router, b_router):
    n, d = x.shape
    n_exp = w_router.shape[1]
    tm = min(ROUTER_TM, n)
    return pl.pallas_call(
        functools.partial(_router_kernel, n_exp=n_exp, top_k=TOP_K),
        grid=(n // tm,),
        in_specs=[pl.BlockSpec((tm, d), lambda i: (i, 0)),
                  pl.BlockSpec((d, n_exp), lambda i: (0, 0)),
                  pl.BlockSpec((1, n_exp), lambda i: (0, 0))],
        out_specs=[pl.BlockSpec((tm, LANES), lambda i: (i, 0)),
                   pl.BlockSpec((tm, LANES), lambda i: (i, 0)),
                   pl.BlockSpec((1, n_exp), lambda i: (0, 0))],
        out_shape=[jax.ShapeDtypeStruct((n, LANES), jnp.int32),
                   jax.ShapeDtypeStruct((n, LANES), F32),
                   jax.ShapeDtypeStruct((1, n_exp), F32)],
        compiler_params=_cparams(("arbitrary",)),
        name="moe_router",
    )(x, w_router, b_router.reshape(1, n_exp))


def _dispatch_kernel(dest_ref, pad_start_ref, pad_cnt_ref, x_ref, xs_hbm, zero_ref, sem, zsem,
                     *, top_k, n_exp, tb):
    i = pl.program_id(0)
    base = i * (tb * top_k)

    def row_copy(r, k):
        return pltpu.make_async_copy(x_ref.at[pl.ds(r, 1)],
                                     xs_hbm.at[pl.ds(dest_ref[base + r * top_k + k], 1)], sem)

    def for_rows(fn):
        def body(g, c):
            for u in range(DMA_UNROLL):
                for k in range(top_k):
                    fn(row_copy(g * DMA_UNROLL + u, k))
            return c
        lax.fori_loop(0, tb // DMA_UNROLL, body, 0)

    for_rows(lambda cp: cp.start())

    @pl.when(i == 0)
    def _():
        zero_ref[...] = jnp.zeros_like(zero_ref)

        def pad_copy(e, r):
            return pltpu.make_async_copy(zero_ref.at[pl.ds(0, 1)],
                                         xs_hbm.at[pl.ds(pad_start_ref[e] + r, 1)], zsem)

        def pad_body(e, carry):
            def issue(r, cc):
                pad_copy(e, r).start()
                return cc
            lax.fori_loop(0, pad_cnt_ref[e], issue, 0)

            def drain(r, cc):
                pad_copy(e, r).wait()
                return cc
            lax.fori_loop(0, pad_cnt_ref[e], drain, 0)
            return carry

        lax.fori_loop(0, n_exp, pad_body, 0)

    pltpu.make_async_copy(xs_hbm.at[pl.ds(0, tb * top_k)], xs_hbm.at[pl.ds(0, tb * top_k)], sem).wait()


def moe_dispatch(x, dest, pad_start, pad_cnt, n_rows):
    n, d = x.shape
    n_exp = pad_start.shape[0]
    tb = min(DISPATCH_TB, n)
    return pl.pallas_call(
        functools.partial(_dispatch_kernel, top_k=TOP_K, n_exp=n_exp, tb=tb),
        grid_spec=pltpu.PrefetchScalarGridSpec(
            num_scalar_prefetch=3,
            grid=(n // tb,),
            in_specs=[pl.BlockSpec((tb, d), lambda i, de, ps, pc: (i, 0))],
            out_specs=pl.BlockSpec(memory_space=pl.ANY),
            scratch_shapes=[pltpu.VMEM((8, d), F32),
                            pltpu.SemaphoreType.DMA(()),
                            pltpu.SemaphoreType.DMA(())],
        ),
        out_shape=jax.ShapeDtypeStruct((n_rows, d), F32),
        compiler_params=_cparams(("arbitrary",)),
        name="moe_dispatch",
    )(dest, pad_start, pad_cnt, x)


def _expert_kernel(blk_e_ref, blk_rows_ref, blk_src_ref,
                   x_ref, wg_ref, wu_ref, bg_ref, bu_ref, wd_ref, bd_ref, o_ref,
                   xb_s, h_s, wg_s, wu_s, wd_s, *, n_f):
    s = pl.program_id(0)
    t = pl.program_id(1)
    rows = blk_rows_ref[s]
    n_sub = MOE_SUPER // MOE_SUB

    def for_sub_blocks(fn):
        for r in range(n_sub):
            @pl.when(r * MOE_SUB < rows)
            def _():
                fn(pl.ds(r * MOE_SUB, MOE_SUB))

    @pl.when((rows > 0) & (t == 0))
    def _():
        def cast_x(sl):
            xb_s[sl, :] = x_ref[sl, :].astype(BF16)
        for_sub_blocks(cast_x)

    @pl.when((rows > 0) & (t < n_f))
    def _():
        wg_s[...] = wg_ref[0, 0].astype(BF16)
        wu_s[...] = wu_ref[0, 0].astype(BF16)

        def up_proj(sl):
            xb = xb_s[sl, :]
            hg = jnp.dot(xb, wg_s[...], preferred_element_type=F32) + bg_ref[0, 0]
            hu = jnp.dot(xb, wu_s[...], preferred_element_type=F32) + bu_ref[0, 0]
            gl = jnp.minimum(hg, SWIGLU_LIMIT)
            up = jnp.clip(hu, -SWIGLU_LIMIT, SWIGLU_LIMIT)
            h_s[t, sl, :] = (gl * jax.nn.sigmoid(SWIGLU_ALPHA * gl) * (up + 1.0)).astype(BF16)
        for_sub_blocks(up_proj)

    @pl.when((rows > 0) & (t >= n_f))
    def _():
        wd_s[...] = wd_ref[0, 0].astype(BF16)

        def down_proj(sl):
            h = jnp.concatenate([h_s[j, sl, :] for j in range(n_f)], axis=1)
            o_ref[sl, :] = jnp.dot(h, wd_s[...], preferred_element_type=F32) + bd_ref[0, 0]
        for_sub_blocks(down_proj)

        for r in range(1, n_sub):
            @pl.when(r * MOE_SUB >= rows)
            def _():
                o_ref[pl.ds(r * MOE_SUB, MOE_SUB), :] = jnp.zeros((MOE_SUB, o_ref.shape[1]), F32)


def moe_experts(xs, blk_e, blk_rows, blk_src, w_gu, b_gu, w_dn, b_dn, layer):
    n_rows, d = xs.shape
    n_layers, n_exp, _, two_f = w_gu.shape
    d_ff = two_f // 2
    n_super = n_rows // MOE_SUPER
    n_f = d_ff // MOE_TF
    n_d = d // MOE_TN

    def j1(s, t, rows):
        return jnp.where(rows[s] > 0, jnp.minimum(t, n_f - 1), n_f - 1)

    def j2(s, t, rows):
        return jnp.where(rows[s] > 0, jnp.maximum(t - n_f, 0), n_d - 1)

    return pl.pallas_call(
        functools.partial(_expert_kernel, n_f=n_f),
        grid_spec=pltpu.PrefetchScalarGridSpec(
            num_scalar_prefetch=3,
            grid=(n_super, n_f + n_d),
            in_specs=[
                pl.BlockSpec((MOE_SUPER, d), lambda s, t, be, br, bs: (bs[s], 0)),
                pl.BlockSpec((1, 1, d, MOE_TF),
                             lambda s, t, be, br, bs: (layer, be[s], 0, j1(s, t, br))),
                pl.BlockSpec((1, 1, d, MOE_TF),
                             lambda s, t, be, br, bs: (layer, be[s], 0, n_f + j1(s, t, br))),
                pl.BlockSpec((1, 1, 1, MOE_TF),
                             lambda s, t, be, br, bs: (layer, be[s], 0, j1(s, t, br))),
                pl.BlockSpec((1, 1, 1, MOE_TF),
                             lambda s, t, be, br, bs: (layer, be[s], 0, n_f + j1(s, t, br))),
                pl.BlockSpec((1, 1, d_ff, MOE_TN),
                             lambda s, t, be, br, bs: (layer, be[s], 0, j2(s, t, br))),
                pl.BlockSpec((1, 1, 1, MOE_TN),
                             lambda s, t, be, br, bs: (layer, be[s], 0, j2(s, t, br))),
            ],
            out_specs=pl.BlockSpec((MOE_SUPER, MOE_TN), lambda s, t, be, br, bs: (bs[s], j2(s, t, br))),
            scratch_shapes=[pltpu.VMEM((MOE_SUPER, d), BF16),
                            pltpu.VMEM((n_f, MOE_SUPER, MOE_TF), BF16),
                            pltpu.VMEM((d, MOE_TF), BF16),
                            pltpu.VMEM((d, MOE_TF), BF16),
                            pltpu.VMEM((d_ff, MOE_TN), BF16)],
        ),
        out_shape=jax.ShapeDtypeStruct((n_rows, d), F32),
        compiler_params=_cparams(("arbitrary", "arbitrary")),
        name="moe_experts",
    )(blk_e, blk_rows, blk_src, xs, w_gu, w_gu, b_gu.reshape(n_layers, n_exp, 1, two_f),
      b_gu.reshape(n_layers, n_exp, 1, two_f), w_dn, b_dn.reshape(n_layers, n_exp, 1, d))


def _combine_kernel(dest_ref, x_ref, tg_ref, g_ref, b_ref, yb_hbm, o_ref, buf, sem,
                    *, top_k, alpha, tb):
    i = pl.program_id(0)
    n_blk = pl.num_programs(0)

    def start_rows(blk, slot):
        base = blk * (tb * top_k)

        def body(g, c):
            for u in range(DMA_UNROLL):
                r = g * DMA_UNROLL + u
                for k in range(top_k):
                    pltpu.make_async_copy(
                        yb_hbm.at[pl.ds(dest_ref[base + r * top_k + k], 1)],
                        buf.at[slot, pl.ds(k * tb + r, 1)], sem.at[slot]).start()
            return c
        lax.fori_loop(0, tb // DMA_UNROLL, body, 0)

    @pl.when(i == 0)
    def _():
        start_rows(0, 0)

    @pl.when(i + 1 < n_blk)
    def _():
        start_rows(i + 1, (i + 1) % 2)

    slot = i % 2
    pltpu.make_async_copy(yb_hbm.at[pl.ds(0, tb * top_k)], buf.at[slot], sem.at[slot]).wait()

    tg = tg_ref[...]
    acc = alpha * x_ref[...]
    for k in range(top_k):
        acc = acc + tg[:, k:k + 1] * buf[slot, k * tb:(k + 1) * tb]
    o_ref[...] = _ln_rows(acc, g_ref[...], b_ref[...])


def moe_combine_ln(x, yb, dest, tg, g, b, alpha):
    n, d = x.shape
    tb = COMBINE_TB
    return pl.pallas_call(
        functools.partial(_combine_kernel, top_k=TOP_K, alpha=alpha, tb=tb),
        grid_spec=pltpu.PrefetchScalarGridSpec(
            num_scalar_prefetch=1,
            grid=(n // tb,),
            in_specs=[pl.BlockSpec((tb, d), lambda i, dr: (i, 0)),
                      pl.BlockSpec((tb, LANES), lambda i, dr: (i, 0)),
                      pl.BlockSpec((1, d), lambda i, dr: (0, 0)),
                      pl.BlockSpec((1, d), lambda i, dr: (0, 0)),
                      pl.BlockSpec(memory_space=pl.ANY)],
            out_specs=pl.BlockSpec((tb, d), lambda i, dr: (i, 0)),
            scratch_shapes=[pltpu.VMEM((2, TOP_K * tb, d), F32),
                            pltpu.SemaphoreType.DMA((2,))],
        ),
        out_shape=jax.ShapeDtypeStruct((n, d), F32),
        compiler_params=_cparams(("arbitrary",)),
        name="moe_combine_ln",
    )(dest, x, tg, g.reshape(1, d), b.reshape(1, d), yb)


def moe_layer(x, w_router, b_router, w_gu, b_gu, w_dn, b_dn, layer, ln_g, ln_b, alpha):
    n, d = x.shape
    n_exp = w_router.shape[1]
    m = n * TOP_K
    n_super = -(-m // MOE_SUPER) + n_exp
    n_rows = n_super * MOE_SUPER

    ti, tg, cnt = moe_router(x, w_router, b_router)
    counts = cnt[0].astype(jnp.int32)
    nsup = (counts + MOE_SUPER - 1) // MOE_SUPER
    sup_end = jnp.cumsum(nsup)
    sup_start = sup_end - nsup
    row_start = sup_start * MOE_SUPER
    top_i = ti[:, :TOP_K]
    rank = ti[:, TOP_K:2 * TOP_K]
    e_ids = jnp.arange(n_exp, dtype=jnp.int32)
    pick = lambda idx, table: jnp.sum(jnp.where(idx[..., None] == e_ids, table, 0), axis=-1)
    dest = (pick(top_i, row_start) + rank).reshape(m)
    s_idx = jnp.arange(n_super, dtype=jnp.int32)
    n_used = sup_end[-1]
    used = s_idx < n_used
    src = jnp.where(used, s_idx, n_used - 1)
    blk_e = jnp.minimum(jnp.sum((src[:, None] >= sup_end[None, :]).astype(jnp.int32), axis=1), n_exp - 1)
    blk_rows = jnp.clip(pick(blk_e, counts) - (src - pick(blk_e, sup_start)) * MOE_SUPER, 0, MOE_SUPER)
    blk_rows = jnp.where(used, blk_rows, 0).astype(jnp.int32)
    pad_start = row_start + counts
    pad_cnt = (-counts) % MOE_SUB

    xs = moe_dispatch(x, dest, pad_start, pad_cnt, n_rows)
    yb = moe_experts(xs, blk_e, blk_rows, src.astype(jnp.int32), w_gu, b_gu, w_dn, b_dn, layer)
    return moe_combine_ln(x, yb, dest, tg, ln_g, ln_b, alpha)


def _head_sum_matrix(n_b, scale):
    row = lax.broadcasted_iota(jnp.int32, (LANES, LANES), 0)
    col = lax.broadcasted_iota(jnp.int32, (LANES, LANES), 1)
    return jnp.where(row // n_b == col // n_b, scale, 0.0).astype(BF16)


def _split_dot(dot_fn, x, terms=3):
    acc = None
    for _ in range(terms):
        piece = x.astype(BF16)
        part = dot_fn(piece)
        acc = part if acc is None else acc + part
        x = x - piece.astype(F32)
    return acc


def _per_head(x, mat):
    outs = [_split_dot(lambda a: jnp.dot(a, mat, preferred_element_type=F32),
                       x[:, s * LANES:(s + 1) * LANES]) for s in range(x.shape[1] // LANES)]
    return jnp.concatenate(outs, axis=1)


def _rwkv_prep_kernel(p_ref, prev_ref, mu_ref, w0_ref, w2_ref, a0_ref, a2_ref, g2_ref, kk_ref,
                      ka_ref, r_ref, k_ref, v_ref, lw_ref, al_ref, be_ref, g_ref,
                      *, d_b, wl, al, n_b):
    p = p_ref[...]
    xm = p + (prev_ref[...] - p) * mu_ref[...]
    r = xm[:, :d_b]
    k = xm[:, d_b:2 * d_b]
    v = xm[:, 2 * d_b:3 * d_b]
    wd = xm[:, 3 * d_b:3 * d_b + wl]
    ad = xm[:, 3 * d_b + wl:3 * d_b + wl + al]
    gd = xm[:, 3 * d_b + wl + al:]
    dotb = lambda a, w_ref: jnp.dot(a.astype(BF16), w_ref[...].astype(BF16),
                                    preferred_element_type=F32)
    wlog = -jax.nn.softplus(-(w0_ref[...] + dotb(jnp.tanh(wd), w2_ref))) - 0.5
    a = jax.nn.sigmoid(a0_ref[...] + dotb(ad, a2_ref))
    kk = k * kk_ref[...]
    norm = jnp.sqrt(_per_head(kk * kk, _head_sum_matrix(n_b, 1.0)))
    kk = kk / jnp.maximum(norm, 1e-12)
    r_ref[...] = r
    k_ref[...] = k * (1.0 + (a - 1.0) * ka_ref[...])
    v_ref[...] = v
    lw_ref[...] = -jnp.exp(wlog)
    al_ref[...] = -kk
    be_ref[...] = kk * a
    g_ref[...] = dotb(jax.nn.sigmoid(gd), g2_ref)


def rwkv_prep(p, row0, prev, mu, w0, w2, a0, a2, g2, k_k, k_a, n_b, tm=256):
    n, pb = prev.shape
    blk0 = row0 // tm
    d_b = w0.shape[0]
    wl, al = w2.shape[0], a2.shape[0]
    row = lambda c: pl.BlockSpec((tm, c), lambda i: (i, 0))
    full = lambda a: pl.BlockSpec(a.shape, lambda i: (0, 0))
    vec = lambda a: a.reshape(1, -1)
    consts = [vec(mu), vec(w0), w2, vec(a0), a2, g2, vec(k_k), vec(k_a)]
    return pl.pallas_call(
        functools.partial(_rwkv_prep_kernel, d_b=d_b, wl=wl, al=al, n_b=n_b),
        grid=(n // tm,),
        in_specs=[pl.BlockSpec((tm, pb), lambda i: (blk0 + i, 0)), row(pb)] + [full(c) for c in consts],
        out_specs=[row(d_b)] * 7,
        out_shape=[jax.ShapeDtypeStruct((n, d_b), F32)] * 7,
        compiler_params=_cparams(("arbitrary",)),
        name="rwkv_prep",
    )(p, prev, *consts)


def _rwkv_chunk_kernel(r_ref, k_ref, v_ref, lw_ref, al_ref, be_ref, g_ref, s0_ref, gng_ref,
                       gnb_ref, rk_ref, o_ref, s_out_ref, st_ref, *, n_b, chunk, seg, n_pairs, n_par):
    c = pl.program_id(1)
    c2 = 2 * chunk
    n_seg = chunk // seg

    @pl.when(c == 0)
    def _():
        st_ref[...] = s0_ref[...]

    left = lambda mat: (lambda a: jnp.dot(mat, a, preferred_element_type=F32))
    right = lambda mat: (lambda a: jnp.dot(a, mat, preferred_element_type=F32))
    dot = lambda a, b: jnp.dot(a.astype(BF16), b.astype(BF16), preferred_element_type=F32)
    dot_nt = lambda a, b: lax.dot_general(a.astype(BF16), b.astype(BF16), (((1,), (1,)), ((), ())),
                                          preferred_element_type=F32)
    dot_tn = lambda a, b: lax.dot_general(a.astype(BF16), b.astype(BF16), (((0,), (0,)), ((), ())),
                                          preferred_element_type=F32)
    lane = lax.broadcasted_iota(jnp.int32, (chunk, LANES), 1)
    head_a = lane < n_b
    stack = lambda x: jnp.concatenate([jnp.where(head_a, x, 0.0), jnp.where(head_a, 0.0, x)], axis=0)
    twice = lambda x: jnp.concatenate([x, x], axis=0)
    row2 = lax.broadcasted_iota(jnp.int32, (c2, c2), 0)
    col2 = lax.broadcasted_iota(jnp.int32, (c2, c2), 1)
    same = (row2 // seg) == (col2 // seg)
    strict = same & (row2 > col2)
    incl = same & (row2 >= col2)
    rowc = lax.broadcasted_iota(jnp.int32, (chunk, chunk), 0)
    colc = lax.broadcasted_iota(jnp.int32, (chunk, chunk), 1)
    cum = jnp.where((rowc >= colc) & (rowc // seg == colc // seg), 1.0, 0.0).astype(BF16)
    rl = lax.broadcasted_iota(jnp.int32, (LANES, LANES), 0)
    cl_ = lax.broadcasted_iota(jnp.int32, (LANES, LANES), 1)
    blockdiag = (rl // n_b) == (cl_ // n_b)
    sum_mat = _head_sum_matrix(n_b, 1.0)
    segs = [slice(q * seg, (q + 1) * seg) for q in range(n_seg)]
    cat0 = lambda parts: parts[0] if len(parts) == 1 else jnp.concatenate(parts, axis=0)

    cl_all = [_split_dot(left(cum), lw_ref[bi]) for bi in range(n_par)]
    ys_all, stat_rows = [], []

    for bi, p in [(bi, p) for bi in range(n_par) for p in range(n_pairs)]:
        sl = slice(p * LANES, (p + 1) * LANES)
        r, k, v = r_ref[bi, :, sl], k_ref[bi, :, sl], v_ref[bi, :, sl]
        lw, al, be = lw_ref[bi, :, sl], al_ref[bi, :, sl], be_ref[bi, :, sl]
        cl = cl_all[bi][:, sl]
        e_neg = jnp.exp(-cl)
        at = al * jnp.exp(cl - lw)
        rt = r * jnp.exp(cl)
        bt = be * e_neg
        kt = k * e_neg
        gram = dot_nt(jnp.concatenate([stack(at), stack(rt)], axis=0),
                      jnp.concatenate([twice(bt), twice(kt)], axis=0))
        nm = jnp.where(strict, gram[:c2, :c2], 0.0)
        aak = jnp.where(strict, gram[:c2, c2:], 0.0)
        arb = jnp.where(incl, gram[c2:, :c2], 0.0)
        ark = jnp.where(incl, gram[c2:, c2:], 0.0)
        ms = [st_ref[bi * n_seg + q, p] for q in range(n_seg)]
        fs = [dot(jnp.concatenate([at[sq], rt[sq]], axis=0), ms[q]) for q, sq in enumerate(segs)]
        a_s0 = cat0([f[:seg] for f in fs])
        r_s0 = cat0([f[seg:] for f in fs])
        vs = stack(v)
        u = stack(a_s0) + dot(aak, vs)
        pw = nm
        n = 1
        while n < seg:
            u = u + dot(pw, u)
            n *= 2
            if n < seg:
                pw = dot(pw, pw)
        ys = stack(r_s0) + dot(jnp.concatenate([arb, ark], axis=1), jnp.concatenate([u, vs], axis=0))
        y = ys[:chunk] + ys[chunk:]
        up = u[:chunk] + u[chunk:]
        ends = [cl[sq.stop - 1:sq.stop, :] for sq in segs]
        to_end = jnp.exp(cat0([jnp.broadcast_to(e, (seg, LANES)) for e in ends]) - cl)
        b_end, k_end = be * to_end, k * to_end
        for q, sq in enumerate(segs):
            upd = dot_tn(jnp.concatenate([b_end[sq], k_end[sq]], axis=0),
                         jnp.concatenate([up[sq], v[sq]], axis=0))
            pc_col = jnp.transpose(jnp.broadcast_to(jnp.exp(ends[q]), (8, LANES)))[:, 0:1]
            st_ref[bi * n_seg + q, p] = pc_col * ms[q] + jnp.where(blockdiag, upd, 0.0)

        ys_all.append((bi, sl, y))
        stat_rows += [y, y * y, r * k * rk_ref[:, sl]]

    stats = _split_dot(right(sum_mat), jnp.concatenate(stat_rows, axis=0))
    for i, (bi, sl, y) in enumerate(ys_all):
        st3 = stats[3 * i * chunk:3 * (i + 1) * chunk]
        mean = st3[:chunk] * (1.0 / n_b)
        var = st3[chunk:c2] * (1.0 / n_b) - mean * mean
        yn = (y - mean) * lax.rsqrt(var + RWKV_GN_EPS) * gng_ref[:, sl] + gnb_ref[:, sl]
        o_ref[bi, :, sl] = (yn + st3[c2:] * v_ref[bi, :, sl]) * g_ref[bi, :, sl]

    @pl.when(c == pl.num_programs(1) - 1)
    def _():
        s_out_ref[...] = st_ref[...]


def rwkv_chunked(r, k, v, lw, al, be, g, s0_bd, gn_g, gn_b, r_k, bsz, t, n_b, chunk):
    n, d_b = r.shape
    n_pairs = d_b // LANES
    seg = min(t, chunk)
    n_seg = chunk // seg
    n_c = t // seg
    if n_seg == 1 and bsz % RWKV_SEQS_PER_STEP == 0:
        n_par = RWKV_SEQS_PER_STEP
        view = (bsz, t, d_b)
        row = pl.BlockSpec((n_par, chunk, d_b), lambda b, c: (b, c, 0))
    else:
        n_par = 1
        view = (1, n, d_b)
        row = pl.BlockSpec((1, chunk, d_b), lambda b, c: (0, b * n_c + c, 0))
    vec = pl.BlockSpec((1, d_b), lambda b, c: (0, 0))
    n_st = n_par * n_seg
    st = pl.BlockSpec((n_st, n_pairs, LANES, LANES), lambda b, c: (b, 0, 0, 0))
    o, m = pl.pallas_call(
        functools.partial(_rwkv_chunk_kernel, n_b=n_b, chunk=chunk, seg=seg, n_pairs=n_pairs,
                          n_par=n_par),
        grid=(bsz // n_st, n_c),
        in_specs=[row] * 7 + [st, vec, vec, vec],
        out_specs=[row, st],
        out_shape=[jax.ShapeDtypeStruct(view, F32), jax.ShapeDtypeStruct(s0_bd.shape, F32)],
        scratch_shapes=[pltpu.VMEM((n_st, n_pairs, LANES, LANES), F32)],
        compiler_params=_cparams(("arbitrary", "arbitrary")),
        name="rwkv_chunked",
    )(*[a.reshape(view) for a in (r, k, v, lw, al, be, g)], s0_bd, gn_g.reshape(1, d_b),
      gn_b.reshape(1, d_b), r_k.reshape(1, d_b))
    return o.reshape(n, d_b), m


def _state_to_blockdiag(s):
    bsz, n_h, n_v, n_k = s.shape
    st = jnp.swapaxes(s, 2, 3).reshape(bsz, n_h // 2, 2, n_k, n_v)
    z = jnp.zeros_like(st[:, :, 0])
    top = jnp.concatenate([st[:, :, 0], z], axis=-1)
    bot = jnp.concatenate([z, st[:, :, 1]], axis=-1)
    return jnp.concatenate([top, bot], axis=-2)


def _blockdiag_to_state(m, n_b):
    bsz, n_pairs = m.shape[:2]
    a = m[:, :, :n_b, :n_b]
    b = m[:, :, n_b:, n_b:]
    st = jnp.stack([a, b], axis=2).reshape(bsz, n_pairs * 2, n_b, n_b)
    return jnp.swapaxes(st, 2, 3)


def rwkv7_group(p_all, row0, bsz, t, shift_prev, s0, mu, w0, w2, a0, a2, g2, k_k, k_a, r_k, gn_g,
                gn_b, chunk):
    pb = p_all.shape[1]
    n_b = s0.shape[2]
    p = p_all[row0:row0 + bsz * t].reshape(bsz, t, pb)
    prev = jnp.concatenate([shift_prev[:, None], p[:, :-1]], axis=1).reshape(bsz * t, pb)
    tm = 256
    r, k, v, lw, al, be, g = rwkv_prep(p_all, row0, prev, mu, w0, w2, a0, a2, g2, k_k, k_a, n_b, tm=tm)
    o, m = rwkv_chunked(r, k, v, lw, al, be, g, _state_to_blockdiag(s0), gn_g, gn_b, r_k,
                        bsz, t, n_b, chunk)
    return o, _blockdiag_to_state(m, n_b), p[:, -1]


def _hgrn_kernel(q_ref, f_ref, i_ref, gate_ref, lb_ref, ng_ref, s0_ref, o_ref, s_out_ref, st_ref,
                 *, n_h, sub, tb):
    c = pl.program_id(1)

    @pl.when(c == 0)
    def _():
        st_ref[...] = s0_ref[0]

    row = lax.broadcasted_iota(jnp.int32, (sub, sub), 0)
    col = lax.broadcasted_iota(jnp.int32, (sub, sub), 1)
    cum = jnp.where(row >= col, 1.0, 0.0).astype(BF16)
    t_idx = lax.broadcasted_iota(jnp.int32, (sub, LANES), 0)
    bf = lambda x: x.astype(BF16)

    def body(j, h0):
        rows = pl.ds(pl.multiple_of(j * sub, sub), sub)
        cols = slice(h0 * LANES, (h0 + HGRN_HEAD_GROUP) * LANES)
        lb_g = lb_ref[:, cols]
        z_g = f_ref[rows, cols]
        logf_g = jnp.log(lb_g + (1.0 - lb_g) * jax.nn.sigmoid(z_g))
        b_g = _split_dot(lambda a: jnp.dot(cum, a, preferred_element_type=F32), logf_g)
        kk_g = (1.0 - lb_g) * jax.nn.sigmoid(-z_g)
        for h in range(h0, h0 + HGRN_HEAD_GROUP):
            sl = slice(h * LANES, (h + 1) * LANES)
            gl = slice((h - h0) * LANES, (h - h0 + 1) * LANES)
            s_mat = st_ref[h]
            q = jax.nn.silu(q_ref[rows, sl])
            kk = kk_g[:, gl]
            v = i_ref[rows, sl]
            b = b_g[:, gl]
            o = jnp.dot(bf(q * jnp.exp(b)), bf(s_mat), preferred_element_type=F32)
            vb = bf(v).astype(F32)
            for s in range(sub):
                dec = jnp.exp(jnp.where(t_idx >= s, b - b[s:s + 1, :], -jnp.inf))
                att = jnp.sum(q * kk[s:s + 1, :] * dec, axis=-1, keepdims=True)
                o = o + bf(att).astype(F32) * vb[s:s + 1, :]
            bl = b[sub - 1:sub, :]
            kd = bf(kk * jnp.exp(bl - b))
            upd = lax.dot_general(kd, bf(v), (((0,), (0,)), ((), ())), preferred_element_type=F32)
            decay = jnp.exp(bl)
            st_ref[h] = jnp.transpose(jnp.broadcast_to(decay, (8, LANES)))[:, 0:1] * s_mat + upd
            o = o * lax.rsqrt(jnp.mean(o * o, axis=-1, keepdims=True) + LN_EPS) * ng_ref[...]
            o_ref[rows, sl] = o * jax.nn.silu(gate_ref[rows, sl])

    for h0 in range(0, n_h, HGRN_HEAD_GROUP):
        def group_body(j, carry, h0=h0):
            body(j, h0)
            return carry
        lax.fori_loop(0, tb // sub, group_body, 0)

    @pl.when(c == pl.num_programs(1) - 1)
    def _():
        s_out_ref[0] = st_ref[...]


def hgrn2(proj, row0, s0, lb, norm_g, bsz, t, out_buf=None):
    n = proj.shape[0]
    n_h, dk, dv = s0.shape[1:]
    d_a = n_h * dk
    sub = min(HGRN_SUB, t)
    tb = min(256, t)
    n_c = t // tb
    blk0 = row0 // tb
    col = lambda which: pl.BlockSpec((tb, d_a), lambda b, c: (blk0 + b * n_c + c, which))
    st = pl.BlockSpec((1, n_h, dk, dv), lambda b, c: (b, 0, 0, 0))
    return _shared_rows_call(
        functools.partial(_hgrn_kernel, n_h=n_h, sub=sub, tb=tb), 7, out_buf,
        grid=(bsz, n_c),
        in_specs=[col(0), col(1), col(2), col(3),
                  pl.BlockSpec((1, d_a), lambda b, c: (0, 0)),
                  pl.BlockSpec((1, dv), lambda b, c: (0, 0)), st],
        out_specs=[col(0), st],
        out_shape=[jax.ShapeDtypeStruct((n, d_a), F32), jax.ShapeDtypeStruct(s0.shape, F32)],
        scratch_shapes=[pltpu.VMEM((n_h, dk, dv), F32)],
        compiler_params=_cparams(("arbitrary", "arbitrary")),
        name="hgrn2",
    )(proj, proj, proj, proj, lb.reshape(1, d_a), norm_g.reshape(1, dv), s0)


def _pool_ln_kernel(x_ref, halo_ref, w_ref, sc_ref, g_ref, b_ref, o_ref, xx_ref,
                    *, windows, start_pos, alpha, tb, halo):
    c = pl.program_id(1)
    d = x_ref.shape[1]
    pc = d // len(windows)
    xx_ref[0:halo, :] = halo_ref[0, 0]
    xx_ref[halo:halo + tb, :] = x_ref[...]
    x = x_ref[...]
    pos = start_pos + c * tb + lax.broadcasted_iota(jnp.int32, (tb, 1), 0)
    ys = []
    for gi, w in enumerate(windows):
        sl = slice(gi * pc, (gi + 1) * pc)
        acc = x[:, sl]
        for back in range(1, w):
            acc = acc + xx_ref[halo - back:halo - back + tb, sl]
        cnt = jnp.minimum(pos + 1, w).astype(F32)
        u = acc / cnt - x[:, sl]
        ys.append(jnp.dot(u.astype(BF16), w_ref[gi], preferred_element_type=F32))
    y = jnp.concatenate(ys, axis=1) * sc_ref[...]
    o_ref[...] = _ln_rows(alpha * x + y, g_ref[...], b_ref[...])


def pool_residual_ln(x, row0, hist, start_pos, w_pool, scale, g, b, alpha, bsz, t, out_buf=None):
    n, d = x.shape
    halo = 16
    tb = min(256, t)
    n_c = t // tb
    blk0 = row0 // tb
    x4 = x[row0:row0 + bsz * t].reshape(bsz, n_c, tb, d)
    first = jnp.concatenate([jnp.zeros((bsz, halo - hist.shape[1], d), F32), hist], axis=1)
    halos = first[:, None]
    if n_c > 1:
        halos = jnp.concatenate([halos, x4[:, :-1, tb - halo:]], axis=1)
    vec = lambda a: pl.BlockSpec((1, d), lambda bi, c: (0, 0))
    rows = pl.BlockSpec((tb, d), lambda bi, c: (blk0 + bi * n_c + c, 0))
    return _shared_rows_call(
        functools.partial(_pool_ln_kernel, windows=POOL_WINDOWS, start_pos=start_pos,
                          alpha=alpha, tb=tb, halo=halo), 6, out_buf,
        grid=(bsz, n_c),
        in_specs=[rows,
                  pl.BlockSpec((1, 1, halo, d), lambda bi, c: (bi, c, 0, 0)),
                  pl.BlockSpec(w_pool.shape, lambda bi, c: (0, 0, 0)),
                  vec(scale), vec(g), vec(b)],
        out_specs=rows,
        out_shape=jax.ShapeDtypeStruct((n, d), F32),
        scratch_shapes=[pltpu.VMEM((halo + tb, d), F32)],
        compiler_params=_cparams(("arbitrary", "arbitrary")),
        name="pool_residual_ln",
    )(x, halos, w_pool.astype(BF16), scale.reshape(1, d), g.reshape(1, d), b.reshape(1, d))


def kernel(x_prompt, x_sample, state_hgrn, state_rwkv, state_rwkv_shift, state_pool, mix_w_in, hgrn_lb, hgrn_norm_g, rwkv_mu, rwkv_w0, rwkv_w2, rwkv_a0, rwkv_a2, rwkv_g2, rwkv_k_k, rwkv_k_a, rwkv_r_k, rwkv_gn_g, rwkv_gn_b, mix_w_out, pool_w, pool_scale, ln1_g, ln1_b, ln2_g, ln2_b, moe_w_router, moe_b_router, moe_w_gu, moe_b_gu, moe_w_dn, moe_b_dn):
    bp, tp, d = x_prompt.shape
    bs, ts, _ = x_sample.shape
    depth = ln1_g.shape[0]
    n_h_a, dk_a, dv_a = state_hgrn.shape[2:]
    n_h_b, n_b = state_rwkv.shape[2:4]
    d_a = n_h_a * dk_a
    pb = state_rwkv_shift.shape[2]
    hist = state_pool.shape[2]
    n_p = bp * tp
    alpha = (2 * depth) ** 0.25

    lb_p = jax.nn.softmax(hgrn_lb, axis=0)
    lb_all = jnp.cumsum(lb_p, axis=0) - lb_p[0]
    x = jnp.concatenate([x_prompt.reshape(n_p, d), x_sample.reshape(bs * ts, d)], axis=0)
    hg_p, hg_s, rw_p, rw_s, sh_p, sh_s, pl_p, pl_s = [], [], [], [], [], [], [], []
    for l in range(depth):
        if l % 2 == 0:
            e = l // 2
            p_a = matmul(x, mix_w_in, e, 0, 4 * d_a)
            p_b = matmul(x, mix_w_in, e, 4 * d_a, pb)
            rw = (rwkv_mu[e], rwkv_w0[e], rwkv_w2[e], rwkv_a0[e], rwkv_a2[e], rwkv_g2[e],
                  rwkv_k_k[e], rwkv_k_a[e], rwkv_r_k[e], rwkv_gn_g[e], rwkv_gn_b[e])
            o_a, o_bs = None, []
            for grp, (lo, hi, bsz, t) in enumerate(((0, n_p, bp, tp), (n_p, n_p + bs * ts, bs, ts))):
                if grp == 0:
                    s_h = jnp.zeros((bsz, n_h_a, dk_a, dv_a), F32)
                    s_r = jnp.zeros((bsz, n_h_b, n_b, n_b), F32)
                    s_s = jnp.zeros((bsz, pb), F32)
                else:
                    s_h, s_r, s_s = state_hgrn[e], state_rwkv[e], state_rwkv_shift[e]
                o_a, n_h = hgrn2(p_a, lo, s_h, lb_all[e], hgrn_norm_g[e], bsz, t, out_buf=o_a)
                o_g, n_r, n_s = rwkv7_group(p_b, lo, bsz, t, s_s, s_r, *rw, chunk=RWKV_CHUNK)
                o_bs.append(o_g)
                (hg_p, hg_s)[grp].append(n_h)
                (rw_p, rw_s)[grp].append(n_r)
                (sh_p, sh_s)[grp].append(n_s)
            o_b = jnp.concatenate(o_bs, axis=0)
            w_out = mix_w_out[e].astype(BF16)
            x = proj_residual_ln(x, [o_a, o_b], [w_out[:d_a], w_out[d_a:]], ln1_g[l], ln1_b[l], alpha)
        else:
            j = l // 2
            x_p, x_s = x[:n_p], x[n_p:]
            hist_p = jnp.zeros((bp, hist, d), F32)
            pl_p.append(jnp.concatenate([hist_p, x_p.reshape(bp, tp, d)], axis=1)[:, -hist:])
            pl_s.append(jnp.concatenate([state_pool[j], x_s.reshape(bs, ts, d)], axis=1)[:, -hist:])
            pool = (pool_w[j], pool_scale[j], ln1_g[l], ln1_b[l], alpha)
            x_new = pool_residual_ln(x, 0, hist_p, 0, *pool, bp, tp)
            x = pool_residual_ln(x, n_p, state_pool[j], PAST_LEN, *pool, bs, ts, out_buf=x_new)
        x = moe_layer(x, moe_w_router[l], moe_b_router[l], moe_w_gu, moe_b_gu, moe_w_dn, moe_b_dn, l,
                      ln2_g[l], ln2_b[l], alpha)
    return (x[:n_p].reshape(bp, tp, d), x[n_p:].reshape(bs, ts, d),
            jnp.stack(hg_p), jnp.stack(hg_s), jnp.stack(rw_p), jnp.stack(rw_s),
            jnp.stack(sh_p), jnp.stack(sh_s), jnp.stack(pl_p), jnp.stack(pl_s))
```

```python
import functools

import jax
import jax.numpy as jnp
from jax import lax
from jax.experimental import pallas as pl
from jax.experimental.pallas import tpu as pltpu

F32 = jnp.float32
BF16 = jnp.bfloat16

POOL_WINDOWS = (2, 4, 8, 16)
TOP_K = 4
SWIGLU_LIMIT = 7.0
SWIGLU_ALPHA = 1.702
LN_EPS = 1e-5
RWKV_GN_EPS = 64e-5
PAST_LEN = 16384

VMEM_LIMIT_BYTES = 56 * 1024 * 1024
LANES = 128

MOE_SUPER = 1280
MOE_SUB = 640
MOE_TF = 256
MOE_TN = 256
ROUTER_TM = 512
COMBINE_TB = 128
HGRN_SUB = 16
HGRN_HEAD_GROUP = 8
RWKV_SEQS_PER_STEP = 1
RWKV_CHUNK = 64
DISPATCH_TB = 512
DMA_UNROLL = 8


def _cparams(sem):
    return pltpu.CompilerParams(dimension_semantics=sem, vmem_limit_bytes=VMEM_LIMIT_BYTES)


def _shared_rows_call(kernel, n_in, out_buf, **kwargs):
    if out_buf is None:
        return pl.pallas_call(kernel, **kwargs)
    body = lambda *refs: kernel(*refs[:n_in], *refs[n_in + 1:])
    kwargs["in_specs"] = list(kwargs["in_specs"]) + [pl.BlockSpec(memory_space=pl.ANY)]
    call = pl.pallas_call(body, input_output_aliases={n_in: 0}, **kwargs)
    return lambda *args: call(*args, out_buf)


def _mm_kernel(x_ref, w_ref, o_ref):
    o_ref[...] = jnp.dot(x_ref[...].astype(BF16), w_ref[0].astype(BF16),
                         preferred_element_type=F32)


def matmul(x, w, layer, col0, n, tm=512, tn=1024):
    m, k = x.shape
    tm = min(tm, m)
    assert col0 % tn == 0
    c0 = col0 // tn
    return pl.pallas_call(
        _mm_kernel,
        grid=(pl.cdiv(n, tn), pl.cdiv(m, tm)),
        in_specs=[pl.BlockSpec((tm, k), lambda j, i: (i, 0)),
                  pl.BlockSpec((1, k, tn), lambda j, i: (layer, 0, c0 + j))],
        out_specs=pl.BlockSpec((tm, tn), lambda j, i: (i, j)),
        out_shape=jax.ShapeDtypeStruct((m, n), F32),
        compiler_params=_cparams(("arbitrary", "arbitrary")),
        name="matmul",
    )(x, w)


def _ln_rows(z, g, b):
    mu = jnp.mean(z, axis=-1, keepdims=True)
    zc = z - mu
    var = jnp.mean(zc * zc, axis=-1, keepdims=True)
    return zc * lax.rsqrt(var + LN_EPS) * g + b


def _proj_ln_kernel(*refs, n_in, alpha):
    x_ref = refs[0]
    a_refs = refs[1:1 + n_in]
    w_refs = refs[1 + n_in:1 + 2 * n_in]
    g_ref, b_ref, o_ref = refs[1 + 2 * n_in:]
    acc = alpha * x_ref[...]
    for a_ref, w_ref in zip(a_refs, w_refs):
        acc = acc + jnp.dot(a_ref[...].astype(BF16), w_ref[...], preferred_element_type=F32)
    o_ref[...] = _ln_rows(acc, g_ref[...], b_ref[...])


def proj_residual_ln(x, acts, weights, g, b, alpha, tm=256):
    m, d = x.shape
    n_in = len(acts)
    in_specs = [pl.BlockSpec((tm, d), lambda i: (i, 0))]
    in_specs += [pl.BlockSpec((tm, a.shape[1]), lambda i: (i, 0)) for a in acts]
    in_specs += [pl.BlockSpec(w.shape, lambda i: (0, 0)) for w in weights]
    in_specs += [pl.BlockSpec((1, d), lambda i: (0, 0))] * 2
    return pl.pallas_call(
        functools.partial(_proj_ln_kernel, n_in=n_in, alpha=alpha),
        grid=(m // tm,),
        in_specs=in_specs,
        out_specs=pl.BlockSpec((tm, d), lambda i: (i, 0)),
        out_shape=jax.ShapeDtypeStruct((m, d), F32),
        compiler_params=_cparams(("arbitrary",)),
        name="proj_residual_ln",
    )(x, *acts, *weights, g.reshape(1, d), b.reshape(1, d))


def _router_kernel(x_ref, w_ref, b_ref, ti_ref, tg_ref, cnt_ref, *, n_exp, top_k):
    i = pl.program_id(0)

    @pl.when(i == 0)
    def _():
        cnt_ref[...] = jnp.zeros_like(cnt_ref)

    logits = jnp.dot(x_ref[...].astype(BF16), w_ref[...].astype(BF16),
                     preferred_element_type=F32) + b_ref[...]
    tm = logits.shape[0]
    lane = lax.broadcasted_iota(jnp.int32, logits.shape, 1)
    work = logits
    firsts, vals, sel = [], [], jnp.zeros(logits.shape, F32)
    for _ in range(top_k):
        mx = jnp.max(work, axis=-1, keepdims=True)
        first = jnp.min(jnp.where(work == mx, lane, n_exp), axis=-1, keepdims=True)
        pick = lane == first
        firsts.append(first)
        vals.append(mx)
        sel = sel + pick.astype(F32)
        work = jnp.where(pick, -jnp.inf, work)
    exps = [jnp.exp(v - vals[0]) for v in vals]
    den = exps[0]
    for e in exps[1:]:
        den = den + e
    row = lax.broadcasted_iota(jnp.int32, (tm, tm), 0)
    col = lax.broadcasted_iota(jnp.int32, (tm, tm), 1)
    tril = jnp.where(row > col, 1.0, 0.0).astype(BF16)
    before = jnp.dot(tril, sel.astype(BF16), preferred_element_type=F32) + cnt_ref[...]
    out_lane = lax.broadcasted_iota(jnp.int32, (tm, LANES), 1)
    ti = jnp.zeros((tm, LANES), jnp.int32)
    tg = jnp.zeros((tm, LANES), F32)
    for k in range(top_k):
        rank_k = jnp.sum(jnp.where(lane == firsts[k], before, 0.0), axis=-1, keepdims=True)
        ti = jnp.where(out_lane == k, firsts[k], ti)
        ti = jnp.where(out_lane == top_k + k, rank_k.astype(jnp.int32), ti)
        tg = jnp.where(out_lane == k, exps[k] / den, tg)
    ti_ref[...] = ti
    tg_ref[...] = tg
    cnt_ref[...] = cnt_ref[...] + jnp.sum(sel, axis=0, keepdims=True)


def moe_router(x, w_router, b_router):
    n, d = x.shape
    n_exp = w_router.shape[1]
    tm = min(ROUTER_TM, n)
    return pl.pallas_call(
        functools.partial(_router_kernel, n_exp=n_exp, top_k=TOP_K),
        grid=(n // tm,),
        in_specs=[pl.BlockSpec((tm, d), lambda i: (i, 0)),
                  pl.BlockSpec((d, n_exp), lambda i: (0, 0)),
                  pl.BlockSpec((1, n_exp), lambda i: (0, 0))],
        out_specs=[pl.BlockSpec((tm, LANES), lambda i: (i, 0)),
                   pl.BlockSpec((tm, LANES), lambda i: (i, 0)),
                   pl.BlockSpec((1, n_exp), lambda i: (0, 0))],
        out_shape=[jax.ShapeDtypeStruct((n, LANES), jnp.int32),
                   jax.ShapeDtypeStruct((n, LANES), F32),
                   jax.ShapeDtypeStruct((1, n_exp), F32)],
        compiler_params=_cparams(("arbitrary",)),
        name="moe_router",
    )(x, w_router, b_router.reshape(1, n_exp))


def _dispatch_kernel(dest_ref, pad_start_ref, pad_cnt_ref, x_ref, xs_hbm, zero_ref, sem, zsem,
                     *, top_k, n_exp, tb):
    i = pl.program_id(0)
    base = i * (tb * top_k)

    def row_copy(r, k):
        return pltpu.make_async_copy(x_ref.at[pl.ds(r, 1)],
                                     xs_hbm.at[pl.ds(dest_ref[base + r * top_k + k], 1)], sem)

    def for_rows(fn):
        def body(g, c):
            for u in range(DMA_UNROLL):
                for k in range(top_k):
                    fn(row_copy(g * DMA_UNROLL + u, k))
            return c
        lax.fori_loop(0, tb // DMA_UNROLL, body, 0)

    for_rows(lambda cp: cp.start())

    @pl.when(i == 0)
    def _():
        zero_ref[...] = jnp.zeros_like(zero_ref)

        def pad_copy(e, r):
            return pltpu.make_async_copy(zero_ref.at[pl.ds(0, 1)],
                                         xs_hbm.at[pl.ds(pad_start_ref[e] + r, 1)], zsem)

        def pad_body(e, carry):
            def issue(r, cc):
                pad_copy(e, r).start()
                return cc
            lax.fori_loop(0, pad_cnt_ref[e], issue, 0)

            def drain(r, cc):
                pad_copy(e, r).wait()
                return cc
            lax.fori_loop(0, pad_cnt_ref[e], drain, 0)
            return carry

        lax.fori_loop(0, n_exp, pad_body, 0)

    pltpu.make_async_copy(xs_hbm.at[pl.ds(0, tb * top_k)], xs_hbm.at[pl.ds(0, tb * top_k)], sem).wait()


def moe_dispatch(x, dest, pad_start, pad_cnt, n_rows):
    n, d = x.shape
    n_exp = pad_start.shape[0]
    tb = min(DISPATCH_TB, n)
    return pl.pallas_call(
        functools.partial(_dispatch_kernel, top_k=TOP_K, n_exp=n_exp, tb=tb),
        grid_spec=pltpu.PrefetchScalarGridSpec(
            num_scalar_prefetch=3,
            grid=(n // tb,),
            in_specs=[pl.BlockSpec((tb, d), lambda i, de, ps, pc: (i, 0))],
            out_specs=pl.BlockSpec(memory_space=pl.ANY),
            scratch_shapes=[pltpu.VMEM((8, d), F32),
                            pltpu.SemaphoreType.DMA(()),
                            pltpu.SemaphoreType.DMA(())],
        ),
        out_shape=jax.ShapeDtypeStruct((n_rows, d), F32),
        compiler_params=_cparams(("arbitrary",)),
        name="moe_dispatch",
    )(dest, pad_start, pad_cnt, x)


def _expert_kernel(blk_e_ref, blk_rows_ref, blk_src_ref,
                   x_ref, wg_ref, wu_ref, bg_ref, bu_ref, wd_ref, bd_ref, o_ref,
                   xb_s, h_s, wg_s, wu_s, wd_s, *, n_f):
    s = pl.program_id(0)
    t = pl.program_id(1)
    rows = blk_rows_ref[s]
    n_sub = MOE_SUPER // MOE_SUB

    def for_sub_blocks(fn):
        for r in range(n_sub):
            @pl.when(r * MOE_SUB < rows)
            def _():
                fn(pl.ds(r * MOE_SUB, MOE_SUB))

    @pl.when((rows > 0) & (t == 0))
    def _():
        def cast_x(sl):
            xb_s[sl, :] = x_ref[sl, :].astype(BF16)
        for_sub_blocks(cast_x)

    @pl.when((rows > 0) & (t < n_f))
    def _():
        wg_s[...] = wg_ref[0, 0].astype(BF16)
        wu_s[...] = wu_ref[0, 0].astype(BF16)

        def up_proj(sl):
            xb = xb_s[sl, :]
            hg = jnp.dot(xb, wg_s[...], preferred_element_type=F32) + bg_ref[0, 0]
            hu = jnp.dot(xb, wu_s[...], preferred_element_type=F32) + bu_ref[0, 0]
            gl = jnp.minimum(hg, SWIGLU_LIMIT)
            up = jnp.clip(hu, -SWIGLU_LIMIT, SWIGLU_LIMIT)
            h_s[t, sl, :] = (gl * jax.nn.sigmoid(SWIGLU_ALPHA * gl) * (up + 1.0)).astype(BF16)
        for_sub_blocks(up_proj)

    @pl.when((rows > 0) & (t >= n_f))
    def _():
        wd_s[...] = wd_ref[0, 0].astype(BF16)

        def down_proj(sl):
            h = jnp.concatenate([h_s[j, sl, :] for j in range(n_f)], axis=1)
            o_ref[sl, :] = jnp.dot(h, wd_s[...], preferred_element_type=F32) + bd_ref[0, 0]
        for_sub_blocks(down_proj)

        for r in range(1, n_sub):
            @pl.when(r * MOE_SUB >= rows)
            def _():
                o_ref[pl.ds(r * MOE_SUB, MOE_SUB), :] = jnp.zeros((MOE_SUB, o_ref.shape[1]), F32)


def moe_experts(xs, blk_e, blk_rows, blk_src, w_gu, b_gu, w_dn, b_dn, layer):
    n_rows, d = xs.shape
    n_layers, n_exp, _, two_f = w_gu.shape
    d_ff = two_f // 2
    n_super = n_rows // MOE_SUPER
    n_f = d_ff // MOE_TF
    n_d = d // MOE_TN

    def j1(s, t, rows):
        return jnp.where(rows[s] > 0, jnp.minimum(t, n_f - 1), n_f - 1)

    def j2(s, t, rows):
        return jnp.where(rows[s] > 0, jnp.maximum(t - n_f, 0), n_d - 1)

    return pl.pallas_call(
        functools.partial(_expert_kernel, n_f=n_f),
        grid_spec=pltpu.PrefetchScalarGridSpec(
            num_scalar_prefetch=3,
            grid=(n_super, n_f + n_d),
            in_specs=[
                pl.BlockSpec((MOE_SUPER, d), lambda s, t, be, br, bs: (bs[s], 0)),
                pl.BlockSpec((1, 1, d, MOE_TF),
                             lambda s, t, be, br, bs: (layer, be[s], 0, j1(s, t, br))),
                pl.BlockSpec((1, 1, d, MOE_TF),
                             lambda s, t, be, br, bs: (layer, be[s], 0, n_f + j1(s, t, br))),
                pl.BlockSpec((1, 1, 1, MOE_TF),
                             lambda s, t, be, br, bs: (layer, be[s], 0, j1(s, t, br))),
                pl.BlockSpec((1, 1, 1, MOE_TF),
                             lambda s, t, be, br, bs: (layer, be[s], 0, n_f + j1(s, t, br))),
                pl.BlockSpec((1, 1, d_ff, MOE_TN),
                             lambda s, t, be, br, bs: (layer, be[s], 0, j2(s, t, br))),
                pl.BlockSpec((1, 1, 1, MOE_TN),
                             lambda s, t, be, br, bs: (layer, be[s], 0, j2(s, t, br))),
            ],
            out_specs=pl.BlockSpec((MOE_SUPER, MOE_TN), lambda s, t, be, br, bs: (bs[s], j2(s, t, br))),
            scratch_shapes=[pltpu.VMEM((MOE_SUPER, d), BF16),
                            pltpu.VMEM((n_f, MOE_SUPER, MOE_TF), BF16),
                            pltpu.VMEM((d, MOE_TF), BF16),
                            pltpu.VMEM((d, MOE_TF), BF16),
                            pltpu.VMEM((d_ff, MOE_TN), BF16)],
        ),
        out_shape=jax.ShapeDtypeStruct((n_rows, d), F32),
        compiler_params=_cparams(("arbitrary", "arbitrary")),
        name="moe_experts",
    )(blk_e, blk_rows, blk_src, xs, w_gu, w_gu, b_gu.reshape(n_layers, n_exp, 1, two_f),
      b_gu.reshape(n_layers, n_exp, 1, two_f), w_dn, b_dn.reshape(n_layers, n_exp, 1, d))


def _combine_kernel(dest_ref, x_ref, tg_ref, g_ref, b_ref, yb_hbm, o_ref, buf, sem,
                    *, top_k, alpha, tb):
    i = pl.program_id(0)
    n_blk = pl.num_programs(0)

    def start_rows(blk, slot):
        base = blk * (tb * top_k)

        def body(g, c):
            for u in range(DMA_UNROLL):
                r = g * DMA_UNROLL + u
                for k in range(top_k):
                    pltpu.make_async_copy(
                        yb_hbm.at[pl.ds(dest_ref[base + r * top_k + k], 1)],
                        buf.at[slot, pl.ds(k * tb + r, 1)], sem.at[slot]).start()
            return c
        lax.fori_loop(0, tb // DMA_UNROLL, body, 0)

    @pl.when(i == 0)
    def _():
        start_rows(0, 0)

    @pl.when(i + 1 < n_blk)
    def _():
        start_rows(i + 1, (i + 1) % 2)

    slot = i % 2
    pltpu.make_async_copy(yb_hbm.at[pl.ds(0, tb * top_k)], buf.at[slot], sem.at[slot]).wait()

    tg = tg_ref[...]
    acc = alpha * x_ref[...]
    for k in range(top_k):
        acc = acc + tg[:, k:k + 1] * buf[slot, k * tb:(k + 1) * tb]
    o_ref[...] = _ln_rows(acc, g_ref[...], b_ref[...])


def moe_combine_ln(x, yb, dest, tg, g, b, alpha):
    n, d = x.shape
    tb = COMBINE_TB
    return pl.pallas_call(
        functools.partial(_combine_kernel, top_k=TOP_K, alpha=alpha, tb=tb),
        grid_spec=pltpu.PrefetchScalarGridSpec(
            num_scalar_prefetch=1,
            grid=(n // tb,),
            in_specs=[pl.BlockSpec((tb, d), lambda i, dr: (i, 0)),
                      pl.BlockSpec((tb, LANES), lambda i, dr: (i, 0)),
                      pl.BlockSpec((1, d), lambda i, dr: (0, 0)),
                      pl.BlockSpec((1, d), lambda i, dr: (0, 0)),
                      pl.BlockSpec(memory_space=pl.ANY)],
            out_specs=pl.BlockSpec((tb, d), lambda i, dr: (i, 0)),
            scratch_shapes=[pltpu.VMEM((2, TOP_K * tb, d), F32),
                            pltpu.SemaphoreType.DMA((2,))],
        ),
        out_shape=jax.ShapeDtypeStruct((n, d), F32),
        compiler_params=_cparams(("arbitrary",)),
        name="moe_combine_ln",
    )(dest, x, tg, g.reshape(1, d), b.reshape(1, d), yb)


def moe_layer(x, w_router, b_router, w_gu, b_gu, w_dn, b_dn, layer, ln_g, ln_b, alpha):
    n, d = x.shape
    n_exp = w_router.shape[1]
    m = n * TOP_K
    n_super = -(-m // MOE_SUPER) + n_exp
    n_rows = n_super * MOE_SUPER

    ti, tg, cnt = moe_router(x, w_router, b_router)
    counts = cnt[0].astype(jnp.int32)
    nsup = (counts + MOE_SUPER - 1) // MOE_SUPER
    sup_end = jnp.cumsum(nsup)
    sup_start = sup_end - nsup
    row_start = sup_start * MOE_SUPER
    top_i = ti[:, :TOP_K]
    rank = ti[:, TOP_K:2 * TOP_K]
    e_ids = jnp.arange(n_exp, dtype=jnp.int32)
    pick = lambda idx, table: jnp.sum(jnp.where(idx[..., None] == e_ids, table, 0), axis=-1)
    dest = (pick(top_i, row_start) + rank).reshape(m)
    s_idx = jnp.arange(n_super, dtype=jnp.int32)
    n_used = sup_end[-1]
    used = s_idx < n_used
    src = jnp.where(used, s_idx, n_used - 1)
    blk_e = jnp.minimum(jnp.sum((src[:, None] >= sup_end[None, :]).astype(jnp.int32), axis=1), n_exp - 1)
    blk_rows = jnp.clip(pick(blk_e, counts) - (src - pick(blk_e, sup_start)) * MOE_SUPER, 0, MOE_SUPER)
    blk_rows = jnp.where(used, blk_rows, 0).astype(jnp.int32)
    pad_start = row_start + counts
    pad_cnt = (-counts) % MOE_SUB

    xs = moe_dispatch(x, dest, pad_start, pad_cnt, n_rows)
    yb = moe_experts(xs, blk_e, blk_rows, src.astype(jnp.int32), w_gu, b_gu, w_dn, b_dn, layer)
    return moe_combine_ln(x, yb, dest, tg, ln_g, ln_b, alpha)


def _head_sum_matrix(n_b, scale):
    row = lax.broadcasted_iota(jnp.int32, (LANES, LANES), 0)
    col = lax.broadcasted_iota(jnp.int32, (LANES, LANES), 1)
    return jnp.where(row // n_b == col // n_b, scale, 0.0).astype(BF16)


def _split_dot(dot_fn, x, terms=3):
    acc = None
    for _ in range(terms):
        piece = x.astype(BF16)
        part = dot_fn(piece)
        acc = part if acc is None else acc + part
        x = x - piece.astype(F32)
    return acc


def _per_head(x, mat):
    outs = [_split_dot(lambda a: jnp.dot(a, mat, preferred_element_type=F32),
                       x[:, s * LANES:(s + 1) * LANES]) for s in range(x.shape[1] // LANES)]
    return jnp.concatenate(outs, axis=1)


def _rwkv_prep_kernel(p_ref, prev_ref, mu_ref, w0_ref, w2_ref, a0_ref, a2_ref, g2_ref, kk_ref,
                      ka_ref, r_ref, k_ref, v_ref, lw_ref, al_ref, be_ref, g_ref,
                      *, d_b, wl, al, n_b):
    p = p_ref[...]
    xm = p + (prev_ref[...] - p) * mu_ref[...]
    r = xm[:, :d_b]
    k = xm[:, d_b:2 * d_b]
    v = xm[:, 2 * d_b:3 * d_b]
    wd = xm[:, 3 * d_b:3 * d_b + wl]
    ad = xm[:, 3 * d_b + wl:3 * d_b + wl + al]
    gd = xm[:, 3 * d_b + wl + al:]
    dotb = lambda a, w_ref: jnp.dot(a.astype(BF16), w_ref[...].astype(BF16),
                                    preferred_element_type=F32)
    wlog = -jax.nn.softplus(-(w0_ref[...] + dotb(jnp.tanh(wd), w2_ref))) - 0.5
    a = jax.nn.sigmoid(a0_ref[...] + dotb(ad, a2_ref))
    kk = k * kk_ref[...]
    norm = jnp.sqrt(_per_head(kk * kk, _head_sum_matrix(n_b, 1.0)))
    kk = kk / jnp.maximum(norm, 1e-12)
    r_ref[...] = r
    k_ref[...] = k * (1.0 + (a - 1.0) * ka_ref[...])
    v_ref[...] = v
    lw_ref[...] = -jnp.exp(wlog)
    al_ref[...] = -kk
    be_ref[...] = kk * a
    g_ref[...] = dotb(jax.nn.sigmoid(gd), g2_ref)


def rwkv_prep(p, row0, prev, mu, w0, w2, a0, a2, g2, k_k, k_a, n_b, tm=256):
    n, pb = prev.shape
    blk0 = row0 // tm
    d_b = w0.shape[0]
    wl, al = w2.shape[0], a2.shape[0]
    row = lambda c: pl.BlockSpec((tm, c), lambda i: (i, 0))
    full = lambda a: pl.BlockSpec(a.shape, lambda i: (0, 0))
    vec = lambda a: a.reshape(1, -1)
    consts = [vec(mu), vec(w0), w2, vec(a0), a2, g2, vec(k_k), vec(k_a)]
    return pl.pallas_call(
        functools.partial(_rwkv_prep_kernel, d_b=d_b, wl=wl, al=al, n_b=n_b),
        grid=(n // tm,),
        in_specs=[pl.BlockSpec((tm, pb), lambda i: (blk0 + i, 0)), row(pb)] + [full(c) for c in consts],
        out_specs=[row(d_b)] * 7,
        out_shape=[jax.ShapeDtypeStruct((n, d_b), F32)] * 7,
        compiler_params=_cparams(("arbitrary",)),
        name="rwkv_prep",
    )(p, prev, *consts)


def _rwkv_chunk_kernel(r_ref, k_ref, v_ref, lw_ref, al_ref, be_ref, g_ref, s0_ref, gng_ref,
                       gnb_ref, rk_ref, o_ref, s_out_ref, st_ref, *, n_b, chunk, seg, n_pairs, n_par):
    c = pl.program_id(1)
    c2 = 2 * chunk
    n_seg = chunk // seg

    @pl.when(c == 0)
    def _():
        st_ref[...] = s0_ref[...]

    left = lambda mat: (lambda a: jnp.dot(mat, a, preferred_element_type=F32))
    right = lambda mat: (lambda a: jnp.dot(a, mat, preferred_element_type=F32))
    dot = lambda a, b: jnp.dot(a.astype(BF16), b.astype(BF16), preferred_element_type=F32)
    dot_nt = lambda a, b: lax.dot_general(a.astype(BF16), b.astype(BF16), (((1,), (1,)), ((), ())),
                                          preferred_element_type=F32)
    dot_tn = lambda a, b: lax.dot_general(a.astype(BF16), b.astype(BF16), (((0,), (0,)), ((), ())),
                                          preferred_element_type=F32)
    lane = lax.broadcasted_iota(jnp.int32, (chunk, LANES), 1)
    head_a = lane < n_b
    stack = lambda x: jnp.concatenate([jnp.where(head_a, x, 0.0), jnp.where(head_a, 0.0, x)], axis=0)
    twice = lambda x: jnp.concatenate([x, x], axis=0)
    row2 = lax.broadcasted_iota(jnp.int32, (c2, c2), 0)
    col2 = lax.broadcasted_iota(jnp.int32, (c2, c2), 1)
    same = (row2 // seg) == (col2 // seg)
    strict = same & (row2 > col2)
    incl = same & (row2 >= col2)
    rowc = lax.broadcasted_iota(jnp.int32, (chunk, chunk), 0)
    colc = lax.broadcasted_iota(jnp.int32, (chunk, chunk), 1)
    cum = jnp.where((rowc >= colc) & (rowc // seg == colc // seg), 1.0, 0.0).astype(BF16)
    rl = lax.broadcasted_iota(jnp.int32, (LANES, LANES), 0)
    cl_ = lax.broadcasted_iota(jnp.int32, (LANES, LANES), 1)
    blockdiag = (rl // n_b) == (cl_ // n_b)
    sum_mat = _head_sum_matrix(n_b, 1.0)
    segs = [slice(q * seg, (q + 1) * seg) for q in range(n_seg)]
    cat0 = lambda parts: parts[0] if len(parts) == 1 else jnp.concatenate(parts, axis=0)

    cl_all = [_split_dot(left(cum), lw_ref[bi]) for bi in range(n_par)]
    ys_all, stat_rows = [], []

    for bi, p in [(bi, p) for bi in range(n_par) for p in range(n_pairs)]:
        sl = slice(p * LANES, (p + 1) * LANES)
        r, k, v = r_ref[bi, :, sl], k_ref[bi, :, sl], v_ref[bi, :, sl]
        lw, al, be = lw_ref[bi, :, sl], al_ref[bi, :, sl], be_ref[bi, :, sl]
        cl = cl_all[bi][:, sl]
        e_neg = jnp.exp(-cl)
        at = al * jnp.exp(cl - lw)
        rt = r * jnp.exp(cl)
        bt = be * e_neg
        kt = k * e_neg
        gram = dot_nt(jnp.concatenate([stack(at), stack(rt)], axis=0),
                      jnp.concatenate([twice(bt), twice(kt)], axis=0))
        nm = jnp.where(strict, gram[:c2, :c2], 0.0)
        aak = jnp.where(strict, gram[:c2, c2:], 0.0)
        arb = jnp.where(incl, gram[c2:, :c2], 0.0)
        ark = jnp.where(incl, gram[c2:, c2:], 0.0)
        ms = [st_ref[bi * n_seg + q, p] for q in range(n_seg)]
        fs = [dot(jnp.concatenate([at[sq], rt[sq]], axis=0), ms[q]) for q, sq in enumerate(segs)]
        a_s0 = cat0([f[:seg] for f in fs])
        r_s0 = cat0([f[seg:] for f in fs])
        vs = stack(v)
        u = stack(a_s0) + dot(aak, vs)
        pw = nm
        n = 1
        while n < seg:
            u = u + dot(pw, u)
            n *= 2
            if n < seg:
                pw = dot(pw, pw)
        ys = stack(r_s0) + dot(jnp.concatenate([arb, ark], axis=1), jnp.concatenate([u, vs], axis=0))
        y = ys[:chunk] + ys[chunk:]
        up = u[:chunk] + u[chunk:]
        ends = [cl[sq.stop - 1:sq.stop, :] for sq in segs]
        to_end = jnp.exp(cat0([jnp.broadcast_to(e, (seg, LANES)) for e in ends]) - cl)
        b_end, k_end = be * to_end, k * to_end
        for q, sq in enumerate(segs):
            upd = dot_tn(jnp.concatenate([b_end[sq], k_end[sq]], axis=0),
                         jnp.concatenate([up[sq], v[sq]], axis=0))
            pc_col = jnp.transpose(jnp.broadcast_to(jnp.exp(ends[q]), (8, LANES)))[:, 0:1]
            st_ref[bi * n_seg + q, p] = pc_col * ms[q] + jnp.where(blockdiag, upd, 0.0)

        ys_all.append((bi, sl, y))
        stat_rows += [y, y * y, r * k * rk_ref[:, sl]]

    stats = _split_dot(right(sum_mat), jnp.concatenate(stat_rows, axis=0))
    for i, (bi, sl, y) in enumerate(ys_all):
        st3 = stats[3 * i * chunk:3 * (i + 1) * chunk]
        mean = st3[:chunk] * (1.0 / n_b)
        var = st3[chunk:c2] * (1.0 / n_b) - mean * mean
        yn = (y - mean) * lax.rsqrt(var + RWKV_GN_EPS) * gng_ref[:, sl] + gnb_ref[:, sl]
        o_ref[bi, :, sl] = (yn + st3[c2:] * v_ref[bi, :, sl]) * g_ref[bi, :, sl]

    @pl.when(c == pl.num_programs(1) - 1)
    def _():
        s_out_ref[...] = st_ref[...]


def rwkv_chunked(r, k, v, lw, al, be, g, s0_bd, gn_g, gn_b, r_k, bsz, t, n_b, chunk):
    n, d_b = r.shape
    n_pairs = d_b // LANES
    seg = min(t, chunk)
    n_seg = chunk // seg
    n_c = t // seg
    if n_seg == 1 and bsz % RWKV_SEQS_PER_STEP == 0:
        n_par = RWKV_SEQS_PER_STEP
        view = (bsz, t, d_b)
        row = pl.BlockSpec((n_par, chunk, d_b), lambda b, c: (b, c, 0))
    else:
        n_par = 1
        view = (1, n, d_b)
        row = pl.BlockSpec((1, chunk, d_b), lambda b, c: (0, b * n_c + c, 0))
    vec = pl.BlockSpec((1, d_b), lambda b, c: (0, 0))
    n_st = n_par * n_seg
    st = pl.BlockSpec((n_st, n_pairs, LANES, LANES), lambda b, c: (b, 0, 0, 0))
    o, m = pl.pallas_call(
        functools.partial(_rwkv_chunk_kernel, n_b=n_b, chunk=chunk, seg=seg, n_pairs=n_pairs,
                          n_par=n_par),
        grid=(bsz // n_st, n_c),
        in_specs=[row] * 7 + [st, vec, vec, vec],
        out_specs=[row, st],
        out_shape=[jax.ShapeDtypeStruct(view, F32), jax.ShapeDtypeStruct(s0_bd.shape, F32)],
        scratch_shapes=[pltpu.VMEM((n_st, n_pairs, LANES, LANES), F32)],
        compiler_params=_cparams(("arbitrary", "arbitrary")),
        name="rwkv_chunked",
    )(*[a.reshape(view) for a in (r, k, v, lw, al, be, g)], s0_bd, gn_g.reshape(1, d_b),
      gn_b.reshape(1, d_b), r_k.reshape(1, d_b))
    return o.reshape(n, d_b), m


def _state_to_blockdiag(s):
    bsz, n_h, n_v, n_k = s.shape
    st = jnp.swapaxes(s, 2, 3).reshape(bsz, n_h // 2, 2, n_k, n_v)
    z = jnp.zeros_like(st[:, :, 0])
    top = jnp.concatenate([st[:, :, 0], z], axis=-1)
    bot = jnp.concatenate([z, st[:, :, 1]], axis=-1)
    return jnp.concatenate([top, bot], axis=-2)


def _blockdiag_to_state(m, n_b):
    bsz, n_pairs = m.shape[:2]
    a = m[:, :, :n_b, :n_b]
    b = m[:, :, n_b:, n_b:]
    st = jnp.stack([a, b], axis=2).reshape(bsz, n_pairs * 2, n_b, n_b)
    return jnp.swapaxes(st, 2, 3)


def rwkv7_group(p_all, row0, bsz, t, shift_prev, s0, mu, w0, w2, a0, a2, g2, k_k, k_a, r_k, gn_g,
                gn_b, chunk):
    pb = p_all.shape[1]
    n_b = s0.shape[2]
    p = p_all[row0:row0 + bsz * t].reshape(bsz, t, pb)
    prev = jnp.concatenate([shift_prev[:, None], p[:, :-1]], axis=1).reshape(bsz * t, pb)
    tm = 256
    r, k, v, lw, al, be, g = rwkv_prep(p_all, row0, prev, mu, w0, w2, a0, a2, g2, k_k, k_a, n_b, tm=tm)
    o, m = rwkv_chunked(r, k, v, lw, al, be, g, _state_to_blockdiag(s0), gn_g, gn_b, r_k,
                        bsz, t, n_b, chunk)
    return o, _blockdiag_to_state(m, n_b), p[:, -1]


def _hgrn_kernel(q_ref, f_ref, i_ref, gate_ref, lb_ref, ng_ref, s0_ref, o_ref, s_out_ref, st_ref,
                 *, n_h, sub, tb):
    c = pl.program_id(1)

    @pl.when(c == 0)
    def _():
        st_ref[...] = s0_ref[0]

    row = lax.broadcasted_iota(jnp.int32, (sub, sub), 0)
    col = lax.broadcasted_iota(jnp.int32, (sub, sub), 1)
    cum = jnp.where(row >= col, 1.0, 0.0).astype(BF16)
    t_idx = lax.broadcasted_iota(jnp.int32, (sub, LANES), 0)
    bf = lambda x: x.astype(BF16)

    def body(j, h0):
        rows = pl.ds(pl.multiple_of(j * sub, sub), sub)
        cols = slice(h0 * LANES, (h0 + HGRN_HEAD_GROUP) * LANES)
        lb_g = lb_ref[:, cols]
        z_g = f_ref[rows, cols]
        logf_g = jnp.log(lb_g + (1.0 - lb_g) * jax.nn.sigmoid(z_g))
        b_g = _split_dot(lambda a: jnp.dot(cum, a, preferred_element_type=F32), logf_g)
        kk_g = (1.0 - lb_g) * jax.nn.sigmoid(-z_g)
        for h in range(h0, h0 + HGRN_HEAD_GROUP):
            sl = slice(h * LANES, (h + 1) * LANES)
            gl = slice((h - h0) * LANES, (h - h0 + 1) * LANES)
            s_mat = st_ref[h]
            q = jax.nn.silu(q_ref[rows, sl])
            kk = kk_g[:, gl]
            v = i_ref[rows, sl]
            b = b_g[:, gl]
            o = jnp.dot(bf(q * jnp.exp(b)), bf(s_mat), preferred_element_type=F32)
            vb = bf(v).astype(F32)
            for s in range(sub):
                dec = jnp.exp(jnp.where(t_idx >= s, b - b[s:s + 1, :], -jnp.inf))
                att = jnp.sum(q * kk[s:s + 1, :] * dec, axis=-1, keepdims=True)
                o = o + bf(att).astype(F32) * vb[s:s + 1, :]
            bl = b[sub - 1:sub, :]
            kd = bf(kk * jnp.exp(bl - b))
            upd = lax.dot_general(kd, bf(v), (((0,), (0,)), ((), ())), preferred_element_type=F32)
            decay = jnp.exp(bl)
            st_ref[h] = jnp.transpose(jnp.broadcast_to(decay, (8, LANES)))[:, 0:1] * s_mat + upd
            o = o * lax.rsqrt(jnp.mean(o * o, axis=-1, keepdims=True) + LN_EPS) * ng_ref[...]
            o_ref[rows, sl] = o * jax.nn.silu(gate_ref[rows, sl])

    for h0 in range(0, n_h, HGRN_HEAD_GROUP):
        def group_body(j, carry, h0=h0):
            body(j, h0)
            return carry
        lax.fori_loop(0, tb // sub, group_body, 0)

    @pl.when(c == pl.num_programs(1) - 1)
    def _():
        s_out_ref[0] = st_ref[...]


def hgrn2(proj, row0, s0, lb, norm_g, bsz, t, out_buf=None):
    n = proj.shape[0]
    n_h, dk, dv = s0.shape[1:]
    d_a = n_h * dk
    sub = min(HGRN_SUB, t)
    tb = min(256, t)
    n_c = t // tb
    blk0 = row0 // tb
    col = lambda which: pl.BlockSpec((tb, d_a), lambda b, c: (blk0 + b * n_c + c, which))
    st = pl.BlockSpec((1, n_h, dk, dv), lambda b, c: (b, 0, 0, 0))
    return _shared_rows_call(
        functools.partial(_hgrn_kernel, n_h=n_h, sub=sub, tb=tb), 7, out_buf,
        grid=(bsz, n_c),
        in_specs=[col(0), col(1), col(2), col(3),
                  pl.BlockSpec((1, d_a), lambda b, c: (0, 0)),
                  pl.BlockSpec((1, dv), lambda b, c: (0, 0)), st],
        out_specs=[col(0), st],
        out_shape=[jax.ShapeDtypeStruct((n, d_a), F32), jax.ShapeDtypeStruct(s0.shape, F32)],
        scratch_shapes=[pltpu.VMEM((n_h, dk, dv), F32)],
        compiler_params=_cparams(("arbitrary", "arbitrary")),
        name="hgrn2",
    )(proj, proj, proj, proj, lb.reshape(1, d_a), norm_g.reshape(1, dv), s0)


def _pool_ln_kernel(x_ref, halo_ref, w_ref, sc_ref, g_ref, b_ref, o_ref, xx_ref,
                    *, windows, start_pos, alpha, tb, halo):
    c = pl.program_id(1)
    d = x_ref.shape[1]
    pc = d // len(windows)
    xx_ref[0:halo, :] = halo_ref[0, 0]
    xx_ref[halo:halo + tb, :] = x_ref[...]
    x = x_ref[...]
    pos = start_pos + c * tb + lax.broadcasted_iota(jnp.int32, (tb, 1), 0)
    ys = []
    for gi, w in enumerate(windows):
        sl = slice(gi * pc, (gi + 1) * pc)
        acc = x[:, sl]
        for back in range(1, w):
            acc = acc + xx_ref[halo - back:halo - back + tb, sl]
        cnt = jnp.minimum(pos + 1, w).astype(F32)
        u = acc / cnt - x[:, sl]
        ys.append(jnp.dot(u.astype(BF16), w_ref[gi], preferred_element_type=F32))
    y = jnp.concatenate(ys, axis=1) * sc_ref[...]
    o_ref[...] = _ln_rows(alpha * x + y, g_ref[...], b_ref[...])


def pool_residual_ln(x, row0, hist, start_pos, w_pool, scale, g, b, alpha, bsz, t, out_buf=None):
    n, d = x.shape
    halo = 16
    tb = min(256, t)
    n_c = t // tb
    blk0 = row0 // tb
    x4 = x[row0:row0 + bsz * t].reshape(bsz, n_c, tb, d)
    first = jnp.concatenate([jnp.zeros((bsz, halo - hist.shape[1], d), F32), hist], axis=1)
    halos = first[:, None]
    if n_c > 1:
        halos = jnp.concatenate([halos, x4[:, :-1, tb - halo:]], axis=1)
    vec = lambda a: pl.BlockSpec((1, d), lambda bi, c: (0, 0))
    rows = pl.BlockSpec((tb, d), lambda bi, c: (blk0 + bi * n_c + c, 0))
    return _shared_rows_call(
        functools.partial(_pool_ln_kernel, windows=POOL_WINDOWS, start_pos=start_pos,
                          alpha=alpha, tb=tb, halo=halo), 6, out_buf,
        grid=(bsz, n_c),
        in_specs=[rows,
                  pl.BlockSpec((1, 1, halo, d), lambda bi, c: (bi, c, 0, 0)),
                  pl.BlockSpec(w_pool.shape, lambda bi, c: (0, 0, 0)),
                  vec(scale), vec(g), vec(b)],
        out_specs=rows,
        out_shape=jax.ShapeDtypeStruct((n, d), F32),
        scratch_shapes=[pltpu.VMEM((halo + tb, d), F32)],
        compiler_params=_cparams(("arbitrary", "arbitrary")),
        name="pool_residual_ln",
    )(x, halos, w_pool.astype(BF16), scale.reshape(1, d), g.reshape(1, d), b.reshape(1, d))


def kernel(x_prompt, x_sample, state_hgrn, state_rwkv, state_rwkv_shift, state_pool, mix_w_in, hgrn_lb, hgrn_norm_g, rwkv_mu, rwkv_w0, rwkv_w2, rwkv_a0, rwkv_a2, rwkv_g2, rwkv_k_k, rwkv_k_a, rwkv_r_k, rwkv_gn_g, rwkv_gn_b, mix_w_out, pool_w, pool_scale, ln1_g, ln1_b, ln2_g, ln2_b, moe_w_router, moe_b_router, moe_w_gu, moe_b_gu, moe_w_dn, moe_b_dn):
    bp, tp, d = x_prompt.shape
    bs, ts, _ = x_sample.shape
    depth = ln1_g.shape[0]
    n_h_a, dk_a, dv_a = state_hgrn.shape[2:]
    n_h_b, n_b = state_rwkv.shape[2:4]
    d_a = n_h_a * dk_a
    pb = state_rwkv_shift.shape[2]
    hist = state_pool.shape[2]
    n_p = bp * tp
    alpha = (2 * depth) ** 0.25

    lb_p = jax.nn.softmax(hgrn_lb, axis=0)
    lb_all = jnp.cumsum(lb_p, axis=0) - lb_p[0]
    x = jnp.concatenate([x_prompt.reshape(n_p, d), x_sample.reshape(bs * ts, d)], axis=0)
    hg_p, hg_s, rw_p, rw_s, sh_p, sh_s, pl_p, pl_s = [], [], [], [], [], [], [], []
    for l in range(depth):
        if l % 2 == 0:
            e = l // 2
            p_a = matmul(x, mix_w_in, e, 0, 4 * d_a)
            p_b = matmul(x, mix_w_in, e, 4 * d_a, pb)
            rw = (rwkv_mu[e], rwkv_w0[e], rwkv_w2[e], rwkv_a0[e], rwkv_a2[e], rwkv_g2[e],
                  rwkv_k_k[e], rwkv_k_a[e], rwkv_r_k[e], rwkv_gn_g[e], rwkv_gn_b[e])
            o_a, o_bs = None, []
            for grp, (lo, hi, bsz, t) in enumerate(((0, n_p, bp, tp), (n_p, n_p + bs * ts, bs, ts))):
                if grp == 0:
                    s_h = jnp.zeros((bsz, n_h_a, dk_a, dv_a), F32)
                    s_r = jnp.zeros((bsz, n_h_b, n_b, n_b), F32)
                    s_s = jnp.zeros((bsz, pb), F32)
                else:
                    s_h, s_r, s_s = state_hgrn[e], state_rwkv[e], state_rwkv_shift[e]
                o_a, n_h = hgrn2(p_a, lo, s_h, lb_all[e], hgrn_norm_g[e], bsz, t, out_buf=o_a)
                o_g, n_r, n_s = rwkv7_group(p_b, lo, bsz, t, s_s, s_r, *rw, chunk=RWKV_CHUNK)
                o_bs.append(o_g)
                (hg_p, hg_s)[grp].append(n_h)
                (rw_p, rw_s)[grp].append(n_r)
                (sh_p, sh_s)[grp].append(n_s)
            o_b = jnp.concatenate(o_bs, axis=0)
            w_out = mix_w_out[e].astype(BF16)
            x = proj_residual_ln(x, [o_a, o_b], [w_out[:d_a], w_out[d_a:]], ln1_g[l], ln1_b[l], alpha)
        else:
            j = l // 2
            x_p, x_s = x[:n_p], x[n_p:]
            hist_p = jnp.zeros((bp, hist, d), F32)
            pl_p.append(jnp.concatenate([hist_p, x_p.reshape(bp, tp, d)], axis=1)[:, -hist:])
            pl_s.append(jnp.concatenate([state_pool[j], x_s.reshape(bs, ts, d)], axis=1)[:, -hist:])
            pool = (pool_w[j], pool_scale[j], ln1_g[l], ln1_b[l], alpha)
            x_new = pool_residual_ln(x, 0, hist_p, 0, *pool, bp, tp)
            x = pool_residual_ln(x, n_p, state_pool[j], PAST_LEN, *pool, bs, ts, out_buf=x_new)
        x = moe_layer(x, moe_w_router[l], moe_b_router[l], moe_w_gu, moe_b_gu, moe_w_dn, moe_b_dn, l,
                      ln2_g[l], ln2_b[l], alpha)
    return (x[:n_p].reshape(bp, tp, d), x[n_p:].reshape(bs, ts, d),
            jnp.stack(hg_p), jnp.stack(hg_s), jnp.stack(rw_p), jnp.stack(rw_s),
            jnp.stack(sh_p), jnp.stack(sh_s), jnp.stack(pl_p), jnp.stack(pl_s))
```
